```python
import math
import jax, jax.numpy as jnp
from jax import lax
import numpy as np


D_MODEL = 4096
BATCH = 4
SEQ = 2048
DEPTH = 1
DEC_BATCH = 128
DEC_SEQ = 8
PAST_LEN = 16384
PAGE_SIZE = 128

D_A = D_MODEL // 2
HEAD_A = 64
H_A = D_A // HEAD_A
LORA_W = max(32, int(round(1.8 * D_A ** 0.5 / 32)) * 32)
LORA_A = LORA_W
LORA_G = max(32, int(round(0.6 * D_A ** 0.8 / 32)) * 32)
N_SHIFT = 3 * D_A + LORA_W + LORA_A + LORA_G
GN_EPS = 64e-5
D_B = D_MODEL // 2
DV_B = 512
H_B = D_B // DV_B
DK_B = DV_B // 2
D_QK = H_B * DK_B
N_MLSTM = 2 * D_QK + 2 * D_B + 2 * H_B
GATE_CAP = 15.0
CHUNK = 128
MH_EPS = 1e-6
N_GATE = 2 * D_MODEL
N_IN = N_SHIFT + N_MLSTM + N_GATE
N_EXPERTS = 32
TOP_K = 4
D_FF = D_MODEL
SWIGLU_LIMIT = 7.0
SWIGLU_ALPHA = 1.702
EXPERT_BLOCK = 128
NORM_EPS = 1e-5

kernel_name = 'rwkv7_mlstm_gated_moe_step'


def rms_norm(x, g):
    x32 = x.astype(jnp.float32)
    y = x32 * lax.rsqrt(jnp.mean(x32 * x32, axis=-1, keepdims=True) + NORM_EPS)
    return (y * g.astype(jnp.float32)).astype(x.dtype)


def rwkv7_scan(r, w, k, v, a_vec, b_vec, S0):
    def step(S, inp):
        r_t, w_t, k_t, v_t, a_t, b_t = inp
        sa = jnp.einsum('bhvk,bhk->bhv', S, a_t)
        S = S * w_t[:, :, None, :] + sa[..., None] * b_t[:, :, None, :] + v_t[..., None] * k_t[:, :, None, :]
        return S, jnp.einsum('bhvk,bhk->bhv', S, r_t)
    xs = tuple(jnp.moveaxis(t, 1, 0) for t in (r, w, k, v, a_vec, b_vec))
    S, ys = lax.scan(step, S0, xs)
    return jnp.moveaxis(ys, 0, 1), S


def rwkv7_branch(za, shift_prev, S0, mu, w0, w_up, a0, a_up, g_up, k_k, k_a, r_k, lnx_w, lnx_b):
    B, T, _ = za.shape
    prev = jnp.concatenate([shift_prev[:, None, :].astype(za.dtype), za[:, :-1]], axis=1)
    xs = za + mu * (prev - za)
    r, k, v, xw, xa, xg = jnp.split(xs, [D_A, 2 * D_A, 3 * D_A, 3 * D_A + LORA_W, 3 * D_A + LORA_W + LORA_A], axis=-1)
    w_log = -jax.nn.softplus(-(w0 + jnp.tanh(xw) @ w_up).astype(jnp.float32)) - 0.5
    decay = jnp.exp(-jnp.exp(w_log))
    a = jax.nn.sigmoid((a0 + xa @ a_up).astype(jnp.float32))
    g = jax.nn.sigmoid(xg) @ g_up
    heads = lambda t: t.astype(jnp.float32).reshape(B, T, H_A, HEAD_A)
    kk = heads(k * k_k)
    kk = kk / jnp.maximum(jnp.sqrt(jnp.sum(kk * kk, axis=-1, keepdims=True)), 1e-12)
    k_mod = k.astype(jnp.float32) * (1.0 + (a - 1.0) * k_a)
    r_h, k_h, v_h, a_h = heads(r), heads(k_mod), heads(v), heads(a)
    y, S = rwkv7_scan(r_h, heads(decay), k_h, v_h, -kk, kk * a_h, S0.astype(jnp.float32))
    mean = jnp.mean(y, axis=-1, keepdims=True)
    var = jnp.mean(jnp.square(y - mean), axis=-1, keepdims=True)
    y = ((y - mean) * lax.rsqrt(var + GN_EPS)).reshape(B, T, D_A) * lnx_w + lnx_b
    bonus = jnp.sum(r_h * k_h * r_k, axis=-1, keepdims=True) * v_h
    out = (y + bonus.reshape(B, T, D_A)) * g
    return out.astype(za.dtype), za[:, -1], S


def mlstm_chunkwise(q, k, v, log_i, log_f, C0, n0, m0):
    B, H, T, _ = q.shape
    L = math.gcd(T, CHUNK)
    NC = T // L
    chunks = lambda t: jnp.moveaxis(t.reshape(B, H, NC, L, *t.shape[3:]), 2, 0)
    causal = jnp.tril(jnp.ones((L, L), dtype=bool))

    def step(carry, inp):
        C, n, m = carry
        qc, kc, vc, li, lf = inp
        b = jnp.cumsum(lf, axis=-1)
        log_d = jnp.where(causal, b[..., :, None] - b[..., None, :] + li[..., None, :], -jnp.inf)
        m_inter = m[..., None] + b
        m_t = jnp.maximum(m_inter, jnp.max(log_d, axis=-1))
        s = jnp.einsum('bhtd,bhsd->bhts', qc, kc) * jnp.exp(log_d - m_t[..., None])
        scale = jnp.exp(m_inter - m_t)
        num = jnp.einsum('bhts,bhsv->bhtv', s, vc) + scale[..., None] * jnp.einsum('bhtd,bhdv->bhtv', qc, C)
        den = jnp.sum(s, axis=-1) + scale * jnp.einsum('bhtd,bhd->bht', qc, n)
        h = num / jnp.maximum(jnp.abs(den), jnp.exp(-m_t))[..., None]
        g_end = b[..., -1:] - b + li
        m_new = jnp.maximum(m + b[..., -1], jnp.max(g_end, axis=-1))
        wts = jnp.exp(g_end - m_new[..., None])
        dec = jnp.exp(m + b[..., -1] - m_new)
        C = dec[..., None, None] * C + jnp.einsum('bhs,bhsd,bhsv->bhdv', wts, kc, vc)
        n = dec[..., None] * n + jnp.einsum('bhs,bhsd->bhd', wts, kc)
        return (C, n, m_new), h

    (C, n, m), hs = lax.scan(step, (C0, n0, m0), tuple(chunks(t) for t in (q, k, v, log_i, log_f)))
    h = jnp.moveaxis(hs, 0, 2).reshape(B, H, T, -1)
    return h, C, n, m


def mlstm_branch(zb, C0, n0, m0, b_i, b_f, mh_w):
    B, T, _ = zb.shape
    q, k, v, o, ig, fg = jnp.split(zb, [D_QK, 2 * D_QK, 2 * D_QK + D_B, 2 * D_QK + 2 * D_B, 2 * D_QK + 2 * D_B + H_B], axis=-1)
    heads = lambda t, d: jnp.transpose(t.reshape(B, T, H_B, d), (0, 2, 1, 3)).astype(jnp.float32)
    qh = heads(q, DK_B)
    kh = heads(k, DK_B) * (DK_B ** -0.5)
    vh = heads(v, DV_B)
    cap = lambda t: GATE_CAP * jnp.tanh(t / GATE_CAP)
    log_i = jnp.transpose(cap((ig + b_i).astype(jnp.float32)), (0, 2, 1))
    log_f = jax.nn.log_sigmoid(jnp.transpose(cap((fg + b_f).astype(jnp.float32)), (0, 2, 1)))
    h, C, n, m = mlstm_chunkwise(qh, kh, vh, log_i, log_f, C0.astype(jnp.float32),
                                 n0.astype(jnp.float32), m0.astype(jnp.float32))
    h = jnp.transpose(h, (0, 2, 1, 3))
    h = h * lax.rsqrt(jnp.mean(h * h, axis=-1, keepdims=True) + MH_EPS)
    out = h.reshape(B, T, D_B) * mh_w * jax.nn.sigmoid(o.astype(jnp.float32))
    return out.astype(zb.dtype), C, n, m


def moe_ffn(x2d, w_router, b_router, w_mlp1, b_mlp1, w_mlp2, b_mlp2, layer):
    T, D = x2d.shape
    logits = (x2d @ w_router + b_router).astype(jnp.float32)
    top_vals, top_idx = lax.top_k(logits, TOP_K)
    gates = jax.nn.softmax(top_vals, axis=-1)
    A = T * TOP_K
    flat_e = top_idx.reshape(-1)
    flat_tok = jnp.arange(A, dtype=jnp.int32) // TOP_K
    order = jnp.argsort(flat_e)
    e_sorted = flat_e[order]
    tok_sorted = flat_tok[order]
    gate_sorted = gates.reshape(-1)[order]
    counts = jnp.bincount(flat_e, length=N_EXPERTS)
    padded = (counts + EXPERT_BLOCK - 1) // EXPERT_BLOCK * EXPERT_BLOCK
    start = jnp.cumsum(counts) - counts
    pend = jnp.cumsum(padded)
    pstart = pend - padded
    dest = pstart[e_sorted] + (jnp.arange(A, dtype=jnp.int32) - start[e_sorted])
    NB = -(-A // EXPERT_BLOCK) + N_EXPERTS
    P = NB * EXPERT_BLOCK
    tok_buf = jnp.full((P,), T, dtype=jnp.int32).at[dest].set(tok_sorted)
    block_e = jnp.minimum(jnp.searchsorted(pend, jnp.arange(NB) * EXPERT_BLOCK, side='right'), N_EXPERTS - 1)
    x_pad = jnp.concatenate([x2d, jnp.zeros((1, D), x2d.dtype)], axis=0)
    xb = x_pad[tok_buf].reshape(NB, EXPERT_BLOCK, D)

    def expert_block(args):
        xblk, e = args
        hg = xblk @ w_mlp1[layer, e] + b_mlp1[layer, e]
        gate, lin = hg[:, :D_FF], hg[:, D_FF:]
        gate = jnp.minimum(gate, SWIGLU_LIMIT)
        lin = jnp.clip(lin, -SWIGLU_LIMIT, SWIGLU_LIMIT)
        act = gate * jax.nn.sigmoid(SWIGLU_ALPHA * gate) * (lin + 1.0)
        return act @ w_mlp2[layer, e] + b_mlp2[layer, e]

    yb = lax.map(expert_block, (xb, block_e)).reshape(P, D)
    y_assign = (yb[dest] * gate_sorted[:, None]).astype(x2d.dtype)
    return jnp.zeros_like(x2d).at[tok_sorted].add(y_assign)


def hybrid_layer(x, shift_prev, S0, C0, n0, m0, p, w_mlp1, b_mlp1, w_mlp2, b_mlp2, layer):
    B, T, D = x.shape
    h = rms_norm(x, p['norm_mix'])
    z = h @ p['w_in']
    za, zb, zg = jnp.split(z, [N_SHIFT, N_SHIFT + N_MLSTM], axis=-1)
    ya, shift_new, S_new = rwkv7_branch(za, shift_prev, S0, p['mu_shift'], p['w0'], p['w_up'], p['a0'],
                                        p['a_up'], p['g_up'], p['k_k'], p['k_a'], p['r_k'], p['lnx_w'], p['lnx_b'])
    yb, C_new, n_new, m_new = mlstm_branch(zb, C0, n0, m0, p['b_igate'], p['b_fgate'], p['mh_norm'])
    g_a, g_b = jnp.split(jax.nn.sigmoid(zg + p['b_gate']), 2, axis=-1)
    x = x + (g_a * (ya @ p['p_a']) + g_b * (yb @ p['p_b'])) @ p['w_out']
    h2 = rms_norm(x, p['norm_ffn']).reshape(B * T, D)
    x = x + moe_ffn(h2, p['w_router'], p['b_router'], w_mlp1, b_mlp1, w_mlp2, b_mlp2, layer).reshape(B, T, D)
    return x, (shift_new, S_new, C_new, n_new, m_new)


def setup_inputs(seed: int = 0) -> dict:
    key = jax.random.key(seed)
    ks = iter(jax.random.split(key, 40))

    def nrm(shape, scale):
        return jax.random.normal(next(ks), shape, jnp.float32) * scale

    def unif(shape, lo, hi):
        return jax.random.uniform(next(ks), shape, jnp.float32, lo, hi)

    L = DEPTH
    return {
        'x_prompt': nrm((BATCH, SEQ, D_MODEL), 1.0),
        'x_sample': nrm((DEC_BATCH, DEC_SEQ, D_MODEL), 1.0),
        'state_shift': nrm((L, DEC_BATCH, N_SHIFT), 1.0),
        'state_rwkv': nrm((L, DEC_BATCH, H_A, HEAD_A, HEAD_A), 0.3),
        'state_mlstm_c': nrm((L, DEC_BATCH, H_B, DK_B, DV_B), 0.1),
        'state_mlstm_n': nrm((L, DEC_BATCH, H_B, DK_B), 0.1),
        'state_mlstm_m': nrm((L, DEC_BATCH, H_B), 1.0),
        'norm_mix': 1.0 + nrm((L, D_MODEL), 0.05),
        'w_in': nrm((L, D_MODEL, N_IN), D_MODEL ** -0.5),
        'mu_shift': unif((L, N_SHIFT), 0.0, 1.0),
        'w0': unif((L, D_A), -6.0, -1.0),
        'w_up': nrm((L, LORA_W, D_A), 0.5 * LORA_W ** -0.5),
        'a0': nrm((L, D_A), 0.1),
        'a_up': nrm((L, LORA_A, D_A), LORA_A ** -0.5),
        'g_up': nrm((L, LORA_G, D_A), LORA_G ** -0.5),
        'k_k': 0.85 + nrm((L, D_A), 0.05),
        'k_a': 1.0 + nrm((L, D_A), 0.05),
        'r_k': nrm((L, H_A, HEAD_A), 0.1),
        'lnx_w': 1.0 + nrm((L, D_A), 0.05),
        'lnx_b': nrm((L, D_A), 0.02),
        'b_igate': nrm((L, H_B), 0.5),
        'b_fgate': jnp.linspace(3.0, 6.0, H_B, dtype=jnp.float32)[None, :] + nrm((L, H_B), 0.1),
        'mh_norm': 1.0 + nrm((L, D_B), 0.05),
        'b_gate': nrm((L, N_GATE), 0.1),
        'p_a': nrm((L, D_A, D_MODEL), D_A ** -0.5),
        'p_b': nrm((L, D_B, D_MODEL), D_B ** -0.5),
        'w_out': nrm((L, D_MODEL, D_MODEL), D_MODEL ** -0.5),
        'norm_ffn': 1.0 + nrm((L, D_MODEL), 0.05),
        'w_router': nrm((L, D_MODEL, N_EXPERTS), D_MODEL ** -0.5),
        'b_router': nrm((L, N_EXPERTS), 0.01),
        'w_mlp1': nrm((L, N_EXPERTS, D_MODEL, 2 * D_FF), D_MODEL ** -0.5),
        'b_mlp1': nrm((L, N_EXPERTS, 2 * D_FF), 0.01),
        'w_mlp2': nrm((L, N_EXPERTS, D_FF, D_MODEL), D_FF ** -0.5),
        'b_mlp2': nrm((L, N_EXPERTS, D_MODEL), 0.01),
        'norm_final': 1.0 + nrm((D_MODEL,), 0.05),
    }


def reference(x_prompt, x_sample, state_shift, state_rwkv, state_mlstm_c, state_mlstm_n, state_mlstm_m,
              norm_mix, w_in, mu_shift, w0, w_up, a0, a_up, g_up, k_k, k_a, r_k, lnx_w, lnx_b,
              b_igate, b_fgate, mh_norm, b_gate, p_a, p_b, w_out, norm_ffn, w_router, b_router,
              w_mlp1, b_mlp1, w_mlp2, b_mlp2, norm_final):
    xp, xs = x_prompt, x_sample
    Bp = x_prompt.shape[0]
    new_p = []
    new_s = []
    for l in range(DEPTH):
        p = dict(norm_mix=norm_mix[l], w_in=w_in[l], mu_shift=mu_shift[l], w0=w0[l], w_up=w_up[l],
                 a0=a0[l], a_up=a_up[l], g_up=g_up[l], k_k=k_k[l], k_a=k_a[l], r_k=r_k[l],
                 lnx_w=lnx_w[l], lnx_b=lnx_b[l], b_igate=b_igate[l], b_fgate=b_fgate[l],
                 mh_norm=mh_norm[l], b_gate=b_gate[l], p_a=p_a[l], p_b=p_b[l], w_out=w_out[l],
                 norm_ffn=norm_ffn[l], w_router=w_router[l], b_router=b_router[l])
        xp, sp = hybrid_layer(xp,
                              jnp.zeros((Bp, N_SHIFT), x_prompt.dtype),
                              jnp.zeros((Bp, H_A, HEAD_A, HEAD_A), jnp.float32),
                              jnp.zeros((Bp, H_B, DK_B, DV_B), jnp.float32),
                              jnp.zeros((Bp, H_B, DK_B), jnp.float32),
                              jnp.zeros((Bp, H_B), jnp.float32),
                              p, w_mlp1, b_mlp1, w_mlp2, b_mlp2, l)
        xs, ss = hybrid_layer(xs, state_shift[l], state_rwkv[l], state_mlstm_c[l], state_mlstm_n[l],
                              state_mlstm_m[l], p, w_mlp1, b_mlp1, w_mlp2, b_mlp2, l)
        new_p.append(sp)
        new_s.append(ss)
    y_prompt = rms_norm(xp, norm_final)
    y_sample = rms_norm(xs, norm_final)
    stk = lambda states, i: jnp.stack([st[i] for st in states], axis=0)
    return (y_prompt, y_sample,
            stk(new_p, 0), stk(new_p, 1), stk(new_p, 2), stk(new_p, 3), stk(new_p, 4),
            stk(new_s, 0), stk(new_s, 1), stk(new_s, 2), stk(new_s, 3), stk(new_s, 4))
```

```python
import functools
import math

import jax
import jax.numpy as jnp
from jax import lax
from jax.experimental import pallas as pl
from jax.experimental.pallas import tpu as pltpu

f32 = jnp.float32
bf16 = jnp.bfloat16
i32 = jnp.int32

LANES = 128
HEAD_A = 64
NORM_EPS = 1e-5
GN_EPS = 64e-5
MH_EPS = 1e-6
GATE_CAP = 15.0
TOP_K = 4
SWIGLU_LIMIT = 7.0
SWIGLU_ALPHA = 1.702
RWKV_CHUNK = 64
MLSTM_CHUNK = 128
VMEM_LIMIT = 56 * 1024 * 1024

NN = (((1,), (0,)), ((), ()))
NT = (((1,), (1,)), ((), ()))
TN = (((0,), (0,)), ((), ()))


def _round_up(x, m):
    return (x + m - 1) // m * m


def _mx(x):
    if x.dtype == bf16 or x.shape[0] % 16 != 0:
        return x
    return x.astype(bf16)


def _dot(a, b, dims=NN):
    return lax.dot_general(_mx(a), _mx(b), dims, preferred_element_type=f32)


def _split(x):
    hi = x.astype(bf16)
    lo = (x - hi.astype(f32)).astype(bf16)
    return hi, lo


def _dot3(a, b, dims=NN):
    ah, al = _split(a)
    bh, bl = _split(b)
    d = lambda x, y: lax.dot_general(x, y, dims, preferred_element_type=f32)
    return d(ah, bh) + d(ah, bl) + d(al, bh)


def _dot_hi(a, b, dims=NN):
    return lax.dot_general(a, b, dims, preferred_element_type=f32, precision=lax.Precision.HIGHEST)


def _iota(shape, dim):
    return lax.broadcasted_iota(i32, shape, dim)


def _params(*sem):
    return pltpu.CompilerParams(dimension_semantics=sem, vmem_limit_bytes=VMEM_LIMIT)


def _rmsnorm_kernel(x_ref, g_ref, o_ref):
    x = x_ref[...]
    y = x * lax.rsqrt(jnp.mean(x * x, axis=-1, keepdims=True) + NORM_EPS)
    o_ref[...] = (y * g_ref[...]).astype(o_ref.dtype)


def rmsnorm(x, g, out_dtype, tm=512):
    n, d = x.shape
    return pl.pallas_call(
        _rmsnorm_kernel,
        grid=(n // tm,),
        in_specs=[pl.BlockSpec((tm, d), lambda i: (i, 0)), pl.BlockSpec((1, d), lambda i: (0, 0))],
        out_specs=pl.BlockSpec((tm, d), lambda i: (i, 0)),
        out_shape=jax.ShapeDtypeStruct((n, d), out_dtype),
        compiler_params=_params("parallel"),
        name="rmsnorm",
    )(x, g.reshape(1, d))


def _matmul_kernel(x_ref, w_ref, o_ref):
    o_ref[...] = jnp.dot(x_ref[...], w_ref[...], preferred_element_type=f32).astype(o_ref.dtype)


def matmul(x, w, tn, tm=512, out_dtype=f32):
    n, k = x.shape
    m = w.shape[1]
    return pl.pallas_call(
        _matmul_kernel,
        grid=(m // tn, n // tm),
        in_specs=[pl.BlockSpec((tm, k), lambda j, i: (i, 0)), pl.BlockSpec((k, tn), lambda j, i: (0, j))],
        out_specs=pl.BlockSpec((tm, tn), lambda j, i: (i, j)),
        out_shape=jax.ShapeDtypeStruct((n, m), out_dtype),
        compiler_params=_params("parallel", "parallel"),
        name="matmul",
    )(x, w)


def _cap(t):
    return GATE_CAP * jnp.tanh(t / GATE_CAP)


def _log_sigmoid(x):
    return jnp.minimum(x, 0.0) - jnp.log1p(jnp.exp(-jnp.abs(x)))


def _mlstm_kernel(q_ref, k_ref, v_ref, o_ref, g_ref, bi_ref, bf_ref, mhw_ref, c0_ref, n0_ref, m0_ref,
                  y_ref, c_ref, n_ref, m_ref, *, L, H, DK, DV):
    @pl.when(pl.program_id(1) == 0)
    def _():
        c_ref[...] = c0_ref[...]
        n_ref[...] = n0_ref[...]
        m_ref[...] = m0_ref[...]

    gates = g_ref[...]
    li_all = _cap(gates + bi_ref[...])
    lf_all = _log_sigmoid(_cap(gates + bf_ref[...]))
    causal = _iota((L, L), 1) <= _iota((L, L), 0)
    b_all = _dot_hi(causal.astype(f32), lf_all)
    sel = (_iota((8, LANES), 0) == _iota((8, LANES), 1)).astype(f32)
    li_rows = _dot_hi(sel, li_all, NT)
    b_rows = _dot_hi(sel, b_all, NT)

    for h in range(H):
        q = q_ref[:, h * DK:(h + 1) * DK]
        k = k_ref[:, h * DK:(h + 1) * DK] * (DK ** -0.5)
        v = v_ref[:, h * DV:(h + 1) * DV]
        bcol = b_all[:, H + h:H + h + 1]
        licol = li_all[:, h:h + 1]
        brow = b_rows[H + h:H + h + 1, :]
        lirow = li_rows[h:h + 1, :]
        m_prev = m_ref[0, :, h:h + 1]
        log_d = jnp.where(causal, bcol - brow + lirow, -jnp.inf)
        m_inter = m_prev + bcol
        m_t = jnp.maximum(m_inter, jnp.max(log_d, axis=-1, keepdims=True))
        s = _dot(q, k, NT) * jnp.exp(log_d - m_t)
        scale = jnp.exp(m_inter - m_t)
        c_prev = c_ref[0, h]
        n_prev = n_ref[0, h:h + 1, :]
        num = _dot(s, v) + scale * _dot(q, c_prev)
        den = jnp.sum(s, axis=-1, keepdims=True) + scale * jnp.sum(q * n_prev, axis=-1, keepdims=True)
        hh = num / jnp.maximum(jnp.abs(den), jnp.exp(-m_t))
        b_end = bcol[L - 1:L, :]
        g_end = b_end - bcol + licol
        m_new = jnp.maximum(m_prev + b_end, jnp.max(g_end, axis=0, keepdims=True))
        wts = jnp.exp(g_end - m_new)
        dec = jnp.exp(m_prev + b_end - m_new)
        kw = k * wts
        c_ref[0, h] = dec * c_prev + _dot(kw, v, TN)
        n_ref[0, h:h + 1, :] = dec * n_prev + jnp.sum(kw, axis=0, keepdims=True)
        m_ref[0, :, h:h + 1] = m_new
        hn = hh * lax.rsqrt(jnp.mean(hh * hh, axis=-1, keepdims=True) + MH_EPS)
        gate_o = jax.nn.sigmoid(o_ref[:, h * DV:(h + 1) * DV])
        y_ref[:, h * DV:(h + 1) * DV] = (hn * mhw_ref[:, h * DV:(h + 1) * DV] * gate_o).astype(y_ref.dtype)


def mlstm(zb, row0, B, T, L, bias_i, bias_f, mh_w, c0, n0, m0):
    _, H, DK, DV = c0.shape
    assert 2 * H <= 8 and T % L == 0 and row0 % L == 0 and DV == 2 * DK
    nc = T // L
    r0 = row0 // L
    rows = lambda b, c: r0 + b * nc + c
    kern = functools.partial(_mlstm_kernel, L=L, H=H, DK=DK, DV=DV)
    qk_w, v_w = H * DK, H * DV
    gate_blk = (2 * qk_w + 2 * v_w) // LANES
    return pl.pallas_call(
        kern,
        grid=(B, nc),
        in_specs=[
            pl.BlockSpec((L, qk_w), lambda b, c: (rows(b, c), 0)),
            pl.BlockSpec((L, qk_w), lambda b, c: (rows(b, c), 1)),
            pl.BlockSpec((L, v_w), lambda b, c: (rows(b, c), 1)),
            pl.BlockSpec((L, v_w), lambda b, c: (rows(b, c), 2)),
            pl.BlockSpec((L, LANES), lambda b, c: (rows(b, c), gate_blk)),
            pl.BlockSpec((1, LANES), lambda b, c: (0, 0)),
            pl.BlockSpec((1, LANES), lambda b, c: (0, 0)),
            pl.BlockSpec((1, v_w), lambda b, c: (0, 0)),
            pl.BlockSpec((1, H, DK, DV), lambda b, c: (b, 0, 0, 0)),
            pl.BlockSpec((1, H, DK), lambda b, c: (b, 0, 0)),
            pl.BlockSpec((1, 1, H), lambda b, c: (b, 0, 0)),
        ],
        out_specs=[
            pl.BlockSpec((L, v_w), lambda b, c: (b * nc + c, 0)),
            pl.BlockSpec((1, H, DK, DV), lambda b, c: (b, 0, 0, 0)),
            pl.BlockSpec((1, H, DK), lambda b, c: (b, 0, 0)),
            pl.BlockSpec((1, 1, H), lambda b, c: (b, 0, 0)),
        ],
        out_shape=[
            jax.ShapeDtypeStruct((B * T, v_w), bf16),
            jax.ShapeDtypeStruct((B, H, DK, DV), f32),
            jax.ShapeDtypeStruct((B, H, DK), f32),
            jax.ShapeDtypeStruct((B, 1, H), f32),
        ],
        compiler_params=_params("parallel", "arbitrary"),
        name="mlstm",
    )(zb, zb, zb, zb, zb, bias_i, bias_f, mh_w, c0, n0, m0.reshape(B, 1, H))


def _softplus(x):
    return jnp.maximum(x, 0.0) + jnp.log1p(jnp.exp(-jnp.abs(x)))


def _dot_sel(x, sel):
    xh, xl = _split(x)
    s = sel.astype(bf16)
    return jnp.dot(xh, s, preferred_element_type=f32) + jnp.dot(xl, s, preferred_element_type=f32)


def _head_blocks(gw, scale):
    return jnp.where(_iota((gw, gw), 0) // HEAD_A == _iota((gw, gw), 1) // HEAD_A, scale, 0.0).astype(f32)


def _rwkv_kernel(r_ref, k_ref, v_ref, l_ref, sp_ref, mu_ref, w0_ref, a0_ref, kk_ref, ka_ref, rk_ref,
                 lw_ref, lb_ref, wup_ref, aup_ref, gup_ref, h0_ref,
                 o_ref, h_ref,
                 last_r, last_k, last_v, last_l, at_sc, bt_sc, kt_sc, rt_sc, bh_sc, kh_sc, v_sc, gam_sc,
                 y_sc, g_sc, bonus_sc, *, L, DA, WP, AP, TLW):
    NP = DA // LANES
    GW = min(2 * LANES, DA)

    @pl.when(pl.program_id(1) == 0)
    def _():
        last_r[...] = sp_ref[0, :, 0:DA]
        last_k[...] = sp_ref[0, :, DA:2 * DA]
        last_v[...] = sp_ref[0, :, 2 * DA:3 * DA]
        last_l[...] = sp_ref[0, :, 3 * DA:3 * DA + TLW]
        h_ref[...] = h0_ref[...]

    def shifted(ref, last, mu):
        cur = ref[...]
        prev = jnp.where(_iota(cur.shape, 0) == 0, last[...], pltpu.roll(cur, 1, 0))
        last[...] = cur[L - 1:L, :]
        return cur + mu * (prev - cur)

    r = shifted(r_ref, last_r, mu_ref[:, 0:DA])
    k = shifted(k_ref, last_k, mu_ref[:, DA:2 * DA])
    v = shifted(v_ref, last_v, mu_ref[:, 2 * DA:3 * DA])
    xl = shifted(l_ref, last_l, mu_ref[:, 3 * DA:3 * DA + TLW])
    xw, xa, xg = xl[:, 0:WP], xl[:, WP:WP + AP], xl[:, WP + AP:]

    w_log = -_softplus(-(w0_ref[...] + _dot3(jnp.tanh(xw), wup_ref[...]))) - 0.5
    logw = -jnp.exp(w_log)
    a = jax.nn.sigmoid(a0_ref[...] + _dot3(xa, aup_ref[...]))
    g_sc[...] = _dot(jax.nn.sigmoid(xg), gup_ref[...])

    ones_bd = _head_blocks(GW, 1.0)
    seg_sum = lambda x: jnp.concatenate(
        [_dot_sel(x[:, i * GW:(i + 1) * GW], ones_bd) for i in range(DA // GW)], axis=1)
    kk = k * kk_ref[...]
    kk = kk / jnp.maximum(jnp.sqrt(seg_sum(kk * kk)), 1e-12)
    k_mod = k * (1.0 + (a - 1.0) * ka_ref[...])
    bonus_sc[...] = seg_sum(r * k_mod * rk_ref[...]) * v

    tri = (_iota((L, L), 1) <= _iota((L, L), 0)).astype(f32)
    cs = _dot_hi(tri, logw)
    cs_end = cs[L - 1:L, :]
    e_neg = jnp.exp(-cs)
    e_end = jnp.exp(cs_end - cs)
    bv = kk * a
    vals = (
        (at_sc, -kk * jnp.exp(cs - logw)), (bt_sc, bv * e_neg), (kt_sc, k_mod * e_neg), (rt_sc, r * jnp.exp(cs)),
        (bh_sc, bv * e_end), (kh_sc, k_mod * e_end), (v_sc, v),
    )
    gam = jnp.exp(cs_end)
    for p in range(NP):
        sl = slice(p * LANES, (p + 1) * LANES)
        for ref, val in vals:
            ref[p] = val[:, sl]
        gam_sc[p] = gam[:, sl]

    lane = _iota((1, LANES), 1)
    m0 = (lane < HEAD_A).astype(f32)
    m1 = 1.0 - m0
    stack = lambda x: jnp.concatenate([x * m0, x * m1], axis=0)
    ri = _iota((2 * L, 2 * L), 0)
    ci = _iota((2 * L, 2 * L), 1)
    same_head = (ri >= L) == (ci >= L)
    strict = same_head & (ci < ri)
    incl = same_head & (ci <= ri)
    eye = (ri == ci).astype(f32)
    eye_l = _iota((LANES, LANES), 0) == _iota((LANES, LANES), 1)
    n_double = int(math.log2(L)) - 1

    def pair_body(p, carry):
        la, lr = stack(at_sc[p]), stack(rt_sc[p])
        bt, kt, vs = stack(bt_sc[p]), stack(kt_sc[p]), stack(v_sc[p])
        lar = jnp.concatenate([la, lr], axis=0)
        pb = _dot3(lar, bt, NT)
        pk = _dot3(lar, kt, NT)
        n_ab = jnp.where(strict, pb[:2 * L], 0.0)
        a_ak = jnp.where(strict, pk[:2 * L], 0.0)
        r_b = jnp.where(incl, pb[2 * L:], 0.0)
        r_k = jnp.where(incl, pk[2 * L:], 0.0)
        hbd = h_ref[0, p]
        xh = _dot3(lar, hbd)
        w = xh[:2 * L] + _dot3(a_ak, vs)
        t_inv = eye + n_ab
        n_pow = n_ab
        for _ in range(n_double):
            n_pow = _dot3(n_pow, n_pow)
            t_inv = t_inv + _dot3(n_pow, t_inv)
        u = _dot3(t_inv, w)
        y_st = xh[2 * L:] + _dot3(r_b, u) + _dot3(r_k, vs)
        y_sc[p] = y_st[:L] + y_st[L:]
        dg = jnp.where(eye_l, gam_sc[p], 0.0)
        lhs = jnp.concatenate([stack(bh_sc[p]), stack(kh_sc[p]), dg], axis=0)
        rhs = jnp.concatenate([u, vs, hbd], axis=0)
        h_ref[0, p] = _dot3(lhs, rhs, TN)
        return carry

    lax.fori_loop(0, NP, pair_body, 0)

    avg_bd = _head_blocks(GW, 1.0 / HEAD_A)
    for i in range(DA // GW):
        sl = slice(i * GW, (i + 1) * GW)
        y = jnp.concatenate([y_sc[i * (GW // LANES) + j] for j in range(GW // LANES)], axis=1)
        d = y - _dot_sel(y, avg_bd)
        yn = d * lax.rsqrt(_dot_sel(d * d, avg_bd) + GN_EPS)
        out = (yn * lw_ref[:, sl] + lb_ref[:, sl] + bonus_sc[:, sl]) * g_sc[:, sl]
        o_ref[:, sl] = out.astype(o_ref.dtype)


def rwkv(za, row0, B, T, L, shift_prev, h0, mu, w0, a0, k_k, k_a, r_k, lnx_w, lnx_b, wup, aup, gup):
    DA = w0.shape[-1]
    WP, AP = wup.shape[0], aup.shape[0]
    TLW = WP + AP + gup.shape[0]
    NA = 3 * DA + TLW
    assert za.shape[1] == NA and (3 * DA) % TLW == 0 and T % L == 0 and row0 % L == 0 and L & (L - 1) == 0
    NP = DA // LANES
    nc = T // L
    r0 = row0 // L
    rows = lambda b, c: r0 + b * nc + c
    kern = functools.partial(_rwkv_kernel, L=L, DA=DA, WP=WP, AP=AP, TLW=TLW)
    vec = pl.BlockSpec((1, DA), lambda b, c: (0, 0))
    full = lambda arr: pl.BlockSpec(arr.shape, lambda b, c: (0,) * arr.ndim)
    pair_sc = pltpu.VMEM((NP, L, LANES), f32)
    return pl.pallas_call(
        kern,
        grid=(B, nc),
        in_specs=[
            pl.BlockSpec((L, DA), lambda b, c: (rows(b, c), 0)),
            pl.BlockSpec((L, DA), lambda b, c: (rows(b, c), 1)),
            pl.BlockSpec((L, DA), lambda b, c: (rows(b, c), 2)),
            pl.BlockSpec((L, TLW), lambda b, c: (rows(b, c), 3 * DA // TLW)),
            pl.BlockSpec((1, 1, NA), lambda b, c: (b, 0, 0)),
            pl.BlockSpec((1, NA), lambda b, c: (0, 0)),
            vec, vec, vec, vec, vec, vec, vec,
            full(wup), full(aup), full(gup),
            pl.BlockSpec((1, NP, LANES, LANES), lambda b, c: (b, 0, 0, 0)),
        ],
        out_specs=[
            pl.BlockSpec((L, DA), lambda b, c: (b * nc + c, 0)),
            pl.BlockSpec((1, NP, LANES, LANES), lambda b, c: (b, 0, 0, 0)),
        ],
        out_shape=[
            jax.ShapeDtypeStruct((B * T, DA), bf16),
            jax.ShapeDtypeStruct((B, NP, LANES, LANES), f32),
        ],
        scratch_shapes=[
            pltpu.VMEM((1, DA), f32), pltpu.VMEM((1, DA), f32), pltpu.VMEM((1, DA), f32), pltpu.VMEM((1, TLW), f32),
            pair_sc, pair_sc, pair_sc, pair_sc, pair_sc, pair_sc, pair_sc, pltpu.VMEM((NP, 1, LANES), f32),
            pair_sc, pltpu.VMEM((L, DA), f32), pltpu.VMEM((L, DA), f32),
        ],
        compiler_params=_params("arbitrary", "arbitrary"),
        name="rwkv",
    )(za, za, za, za, shift_prev, mu, w0, a0, k_k, k_a, r_k, lnx_w, lnx_b, wup, aup, gup, h0)


def _state_to_blockdiag(s):
    B, H = s.shape[:2]
    st = jnp.swapaxes(s, 2, 3).reshape(B, H // 2, 2, HEAD_A, 1, HEAD_A)
    eye = jnp.eye(2, dtype=s.dtype).reshape(1, 1, 2, 1, 2, 1)
    return (st * eye).reshape(B, H // 2, LANES, LANES)


def _blockdiag_to_state(hbd):
    B, NP = hbd.shape[:2]
    h6 = hbd.reshape(B, NP, 2, HEAD_A, 2, HEAD_A)
    st = jnp.stack([h6[:, :, 0, :, 0, :], h6[:, :, 1, :, 1, :]], axis=2).reshape(B, NP * 2, HEAD_A, HEAD_A)
    return jnp.swapaxes(st, 2, 3)


def _merge_kernel(ya_ref, yb_ref, pa_ref, pb_ref, ga_ref, gb_ref, ba_ref, bb_ref, o_ref):
    ga = jax.nn.sigmoid(ga_ref[...] + ba_ref[...])
    gb = jax.nn.sigmoid(gb_ref[...] + bb_ref[...])
    pa = jnp.dot(ya_ref[...], pa_ref[...], preferred_element_type=f32)
    pb = jnp.dot(yb_ref[...], pb_ref[...], preferred_element_type=f32)
    o_ref[...] = (ga * pa + gb * pb).astype(o_ref.dtype)


def merge(ya, yb, p_a, p_b, zg, b_gate, tm=512, tn=512):
    n, da = ya.shape
    db = yb.shape[1]
    d = p_a.shape[1]
    nj = d // tn
    return pl.pallas_call(
        _merge_kernel,
        grid=(nj, n // tm),
        in_specs=[
            pl.BlockSpec((tm, da), lambda j, i: (i, 0)),
            pl.BlockSpec((tm, db), lambda j, i: (i, 0)),
            pl.BlockSpec((da, tn), lambda j, i: (0, j)),
            pl.BlockSpec((db, tn), lambda j, i: (0, j)),
            pl.BlockSpec((tm, tn), lambda j, i: (i, j)),
            pl.BlockSpec((tm, tn), lambda j, i: (i, nj + j)),
            pl.BlockSpec((1, tn), lambda j, i: (0, j)),
            pl.BlockSpec((1, tn), lambda j, i: (0, nj + j)),
        ],
        out_specs=pl.BlockSpec((tm, tn), lambda j, i: (i, j)),
        out_shape=jax.ShapeDtypeStruct((n, d), bf16),
        compiler_params=_params("parallel", "parallel"),
        name="merge",
    )(ya, yb, p_a, p_b, zg, zg, b_gate, b_gate)


def _outproj_kernel(u_ref, w_ref, x_ref, o_ref):
    o_ref[...] = x_ref[...] + jnp.dot(u_ref[...], w_ref[...], preferred_element_type=f32)


def outproj(u, w, x, tm=512, tn=512):
    n, k = u.shape
    d = w.shape[1]
    return pl.pallas_call(
        _outproj_kernel,
        grid=(d // tn, n // tm),
        in_specs=[
            pl.BlockSpec((tm, k), lambda j, i: (i, 0)),
            pl.BlockSpec((k, tn), lambda j, i: (0, j)),
            pl.BlockSpec((tm, tn), lambda j, i: (i, j)),
        ],
        out_specs=pl.BlockSpec((tm, tn), lambda j, i: (i, j)),
        out_shape=jax.ShapeDtypeStruct((n, d), f32),
        compiler_params=_params("parallel", "parallel"),
        name="outproj",
    )(u, w, x)


def _router_kernel(x_ref, g_ref, wr_ref, br_ref, h_ref, idx_ref, gate_ref, *, E):
    x = x_ref[...]
    h = x * lax.rsqrt(jnp.mean(x * x, axis=-1, keepdims=True) + NORM_EPS) * g_ref[...]
    h_ref[...] = h
    logits = _dot_hi(h, wr_ref[...]) + br_ref[...]
    lane = _iota(logits.shape, 1)
    l = jnp.where(lane < E, logits, -jnp.inf)
    vals, idxs = [], []
    for _ in range(TOP_K):
        mx = jnp.max(l, axis=-1, keepdims=True)
        ix = jnp.min(jnp.where(l == mx, lane, LANES), axis=-1, keepdims=True)
        vals.append(mx)
        idxs.append(ix)
        l = jnp.where(lane == ix, -jnp.inf, l)
    es = [jnp.exp(v - vals[0]) for v in vals]
    tot = functools.reduce(lambda a, b: a + b, es)
    gate_out = jnp.zeros(logits.shape, f32)
    idx_out = jnp.zeros(logits.shape, i32)
    for k in range(TOP_K):
        gate_out = jnp.where(lane == k, es[k] / tot, gate_out)
        idx_out = jnp.where(lane == k, idxs[k], idx_out)
    gate_ref[...] = gate_out
    idx_ref[...] = idx_out


def router(x, g, w_router, b_router, tm=256):
    n, d = x.shape
    E = w_router.shape[1]
    wr = jnp.pad(w_router, ((0, 0), (0, LANES - E)))
    br = jnp.pad(b_router, (0, LANES - E)).reshape(1, LANES)
    return pl.pallas_call(
        functools.partial(_router_kernel, E=E),
        grid=(n // tm,),
        in_specs=[
            pl.BlockSpec((tm, d), lambda i: (i, 0)),
            pl.BlockSpec((1, d), lambda i: (0, 0)),
            pl.BlockSpec((d, LANES), lambda i: (0, 0)),
            pl.BlockSpec((1, LANES), lambda i: (0, 0)),
        ],
        out_specs=[
            pl.BlockSpec((tm, d), lambda i: (i, 0)),
            pl.BlockSpec((tm, LANES), lambda i: (i, 0)),
            pl.BlockSpec((tm, LANES), lambda i: (i, 0)),
        ],
        out_shape=[
            jax.ShapeDtypeStruct((n, d), f32),
            jax.ShapeDtypeStruct((n, LANES), i32),
            jax.ShapeDtypeStruct((n, LANES), f32),
        ],
        compiler_params=_params("parallel"),
        name="router",
    )(x, g.reshape(1, d), wr, br)


def _route(idx, E, tb):
    n, k = idx.shape
    onehot = jnp.sum((idx[:, :, None] == jnp.arange(E, dtype=i32)[None, None, :]).astype(i32), axis=1)
    pos = jnp.cumsum(onehot, axis=0) - onehot
    counts = jnp.sum(onehot, axis=0)
    padded = (counts + tb - 1) // tb * tb
    pend = jnp.cumsum(padded)
    pstart = pend - padded
    dest = jnp.take_along_axis(pstart[None, :] + pos, idx, axis=1).astype(i32)
    nb = n * k // tb + E
    tok = jnp.zeros((nb * tb,), i32).at[dest.reshape(-1)].set(jnp.repeat(jnp.arange(n, dtype=i32), k))
    block_e = jnp.minimum(jnp.searchsorted(pend, jnp.arange(nb, dtype=i32) * tb, side="right"), E - 1).astype(i32)
    n_used = (pend[-1] // tb).astype(i32).reshape(1)
    return dest, tok, block_e, n_used


def _gather_kernel(tok_ref, src_hbm, o_ref, buf, sem, *, RB):
    base = pl.program_id(0) * RB

    def issue(r, carry):
        pltpu.make_async_copy(src_hbm.at[pl.ds(tok_ref[base + r], 1)], buf.at[pl.ds(r, 1)], sem).start()
        return carry

    lax.fori_loop(0, RB, issue, 0)
    pltpu.make_async_copy(src_hbm.at[pl.ds(0, RB)], buf, sem).wait()
    o_ref[...] = buf[...].astype(o_ref.dtype)


def gather_rows(src, tok, rb=256):
    p = tok.shape[0]
    d = src.shape[1]
    return pl.pallas_call(
        functools.partial(_gather_kernel, RB=rb),
        grid_spec=pltpu.PrefetchScalarGridSpec(
            num_scalar_prefetch=1,
            grid=(p // rb,),
            in_specs=[pl.BlockSpec(memory_space=pl.ANY)],
            out_specs=pl.BlockSpec((rb, d), lambda i, tok: (i, 0)),
            scratch_shapes=[pltpu.VMEM((rb, d), src.dtype), pltpu.SemaphoreType.DMA(())],
        ),
        out_shape=jax.ShapeDtypeStruct((p, d), bf16),
        compiler_params=_params("arbitrary"),
        name="gather_rows",
    )(tok, src)


def _needs_cast(be_ref, i):
    return jnp.logical_or(i == 0, be_ref[i] != be_ref[jnp.maximum(i - 1, 0)])


def _expert_up_kernel(be_ref, nu_ref, x_ref, wg_ref, wl_ref, bg_ref, bl_ref, o_ref, wg_bf, wl_bf):
    i = pl.program_id(1)

    @pl.when(_needs_cast(be_ref, i))
    def _():
        wg_bf[...] = wg_ref[...].astype(bf16)
        wl_bf[...] = wl_ref[...].astype(bf16)

    @pl.when(i < nu_ref[0])
    def _():
        x = x_ref[...]
        gate = jnp.dot(x, wg_bf[...], preferred_element_type=f32) + bg_ref[...]
        lin = jnp.dot(x, wl_bf[...], preferred_element_type=f32) + bl_ref[...]
        gate = jnp.minimum(gate, SWIGLU_LIMIT)
        lin = jnp.clip(lin, -SWIGLU_LIMIT, SWIGLU_LIMIT)
        o_ref[...] = (gate * jax.nn.sigmoid(SWIGLU_ALPHA * gate) * (lin + 1.0)).astype(o_ref.dtype)

    @pl.when(i >= nu_ref[0])
    def _():
        o_ref[...] = jnp.zeros(o_ref.shape, o_ref.dtype)


def expert_up(xg, w1, b1, block_e, n_used, tb, tf=512):
    p, d = xg.shape
    E, _, f2 = w1.shape
    f = f2 // 2
    nb = p // tb
    nj = f // tf
    return pl.pallas_call(
        _expert_up_kernel,
        grid_spec=pltpu.PrefetchScalarGridSpec(
            num_scalar_prefetch=2,
            grid=(nj, nb),
            in_specs=[
                pl.BlockSpec((tb, d), lambda j, i, be, nu: (i, 0)),
                pl.BlockSpec((None, d, tf), lambda j, i, be, nu: (be[i], 0, j)),
                pl.BlockSpec((None, d, tf), lambda j, i, be, nu: (be[i], 0, nj + j)),
                pl.BlockSpec((None, 1, tf), lambda j, i, be, nu: (be[i], 0, j)),
                pl.BlockSpec((None, 1, tf), lambda j, i, be, nu: (be[i], 0, nj + j)),
            ],
            out_specs=pl.BlockSpec((tb, tf), lambda j, i, be, nu: (i, j)),
            scratch_shapes=[pltpu.VMEM((d, tf), bf16), pltpu.VMEM((d, tf), bf16)],
        ),
        out_shape=jax.ShapeDtypeStruct((p, f), bf16),
        compiler_params=_params("arbitrary", "arbitrary"),
        name="expert_up",
    )(block_e, n_used, xg, w1, w1, b1.reshape(E, 1, f2), b1.reshape(E, 1, f2))


def _expert_down_kernel(be_ref, nu_ref, h_ref, w_ref, b_ref, o_ref, w_bf):
    i = pl.program_id(1)

    @pl.when(_needs_cast(be_ref, i))
    def _():
        w_bf[...] = w_ref[...].astype(bf16)

    @pl.when(i < nu_ref[0])
    def _():
        o_ref[...] = jnp.dot(h_ref[...], w_bf[...], preferred_element_type=f32) + b_ref[...]

    @pl.when(i >= nu_ref[0])
    def _():
        o_ref[...] = jnp.zeros(o_ref.shape, o_ref.dtype)


def expert_down(hid, w2, b2, block_e, n_used, tb, td=512):
    p, f = hid.shape
    E, _, d = w2.shape
    return pl.pallas_call(
        _expert_down_kernel,
        grid_spec=pltpu.PrefetchScalarGridSpec(
            num_scalar_prefetch=2,
            grid=(d // td, p // tb),
            in_specs=[
                pl.BlockSpec((tb, f), lambda j, i, be, nu: (i, 0)),
                pl.BlockSpec((None, f, td), lambda j, i, be, nu: (be[i], 0, j)),
                pl.BlockSpec((None, 1, td), lambda j, i, be, nu: (be[i], 0, j)),
            ],
            out_specs=pl.BlockSpec((tb, td), lambda j, i, be, nu: (i, j)),
            scratch_shapes=[pltpu.VMEM((f, td), bf16)],
        ),
        out_shape=jax.ShapeDtypeStruct((p, d), f32),
        compiler_params=_params("arbitrary", "arbitrary"),
        name="expert_down",
    )(block_e, n_used, hid, w2, b2.reshape(E, 1, d))


def _combine_kernel(dest_ref, x_ref, gate_ref, nf_ref, y_hbm, o_ref, buf, sem, *, TM):
    base = pl.program_id(0) * TM * TOP_K

    def issue(t, carry):
        for k in range(TOP_K):
            row = dest_ref[base + t * TOP_K + k]
            pltpu.make_async_copy(y_hbm.at[pl.ds(row, 1)], buf.at[k, pl.ds(t, 1)], sem).start()
        return carry

    lax.fori_loop(0, TM, issue, 0)
    for k in range(TOP_K):
        pltpu.make_async_copy(y_hbm.at[pl.ds(0, TM)], buf.at[k], sem).wait()
    acc = x_ref[...]
    for k in range(TOP_K):
        acc = acc + gate_ref[:, k:k + 1] * buf[k]
    y = acc * lax.rsqrt(jnp.mean(acc * acc, axis=-1, keepdims=True) + NORM_EPS)
    o_ref[...] = y * nf_ref[...]


def combine(x1, gates, norm_final, yb, dest, tm=128):
    n, d = x1.shape
    return pl.pallas_call(
        functools.partial(_combine_kernel, TM=tm),
        grid_spec=pltpu.PrefetchScalarGridSpec(
            num_scalar_prefetch=1,
            grid=(n // tm,),
            in_specs=[
                pl.BlockSpec((tm, d), lambda i, dest: (i, 0)),
                pl.BlockSpec((tm, LANES), lambda i, dest: (i, 0)),
                pl.BlockSpec((1, d), lambda i, dest: (0, 0)),
                pl.BlockSpec(memory_space=pl.ANY),
            ],
            out_specs=pl.BlockSpec((tm, d), lambda i, dest: (i, 0)),
            scratch_shapes=[pltpu.VMEM((TOP_K, tm, d), f32), pltpu.SemaphoreType.DMA(())],
        ),
        out_shape=jax.ShapeDtypeStruct((n, d), f32),
        compiler_params=_params("arbitrary"),
        name="combine",
    )(dest.reshape(-1), x1, gates, norm_final.reshape(1, d), yb)


def moe_and_final_norm(x1, norm_ffn, w_router, b_router, w1, b1, w2, b2, norm_final, tb, tf=512, td=512,
                       tm_router=256, tm_combine=128):
    h2, idx, gates = router(x1, norm_ffn, w_router, b_router, tm=tm_router)
    dest, tok, block_e, n_used = _route(idx[:, :TOP_K], w_router.shape[1], tb)
    xg = gather_rows(h2, tok, rb=tb)
    hid = expert_up(xg, w1, b1, block_e, n_used, tb, tf=tf)
    yb = expert_down(hid, w2, b2, block_e, n_used, tb, td=td)
    return combine(x1, gates, norm_final, yb, dest, tm=tm_combine)


def _pick_tile(m, cap=1024):
    units = m // LANES
    best = max(u for u in range(1, cap // LANES + 1) if units % u == 0)
    return best * LANES


def _pow2_chunk(t, cap):
    c = 1
    while c * 2 <= cap and t % (c * 2) == 0:
        c *= 2
    return c


def _pad_last(x, width):
    return jnp.pad(x, [(0, 0)] * (x.ndim - 1) + [(0, width - x.shape[-1])])


def kernel(x_prompt, x_sample, state_shift, state_rwkv, state_mlstm_c, state_mlstm_n, state_mlstm_m, norm_mix, w_in, mu_shift, w0, w_up, a0, a_up, g_up, k_k, k_a, r_k, lnx_w, lnx_b, b_igate, b_fgate, mh_norm, b_gate, p_a, p_b, w_out, norm_ffn, w_router, b_router, w_mlp1, b_mlp1, w_mlp2, b_mlp2, norm_final):
    assert norm_mix.shape[0] == 1, "single trunk layer"
    Bp, Tp, D = x_prompt.shape
    Bs, Ts, _ = x_sample.shape
    Np, Ns = Bp * Tp, Bs * Ts
    DA, LW, LA, LG = w0.shape[-1], w_up.shape[1], a_up.shape[1], g_up.shape[1]
    HA = r_k.shape[1]
    assert r_k.shape[2] == HEAD_A and HA * HEAD_A == DA and LG % LANES == 0
    _, _, HB, DK, DV = state_mlstm_c.shape
    DQK, DB = HB * DK, HB * DV
    n_shift = 3 * DA + LW + LA + LG
    n_ml = 2 * DQK + 2 * DB + 2 * HB
    WP, AP = _round_up(LW, LANES), _round_up(LA, LANES)

    def pad_shift_cols(t):
        o = 3 * DA
        return jnp.concatenate(
            [t[..., :o], _pad_last(t[..., o:o + LW], WP), _pad_last(t[..., o + LW:o + LW + LA], AP),
             t[..., o + LW + LA:]], axis=-1)

    def unpad_shift_cols(t):
        o = 3 * DA
        return jnp.concatenate([t[..., :o], t[..., o:o + LW], t[..., o + WP:o + WP + LA], t[..., o + WP + AP:]], axis=-1)

    w = w_in[0]
    wa = pad_shift_cols(w[:, :n_shift]).astype(bf16)
    ob = n_shift + 2 * DQK + 2 * DB
    wb = jnp.concatenate([w[:, n_shift:ob], _pad_last(w[:, ob:n_shift + n_ml], LANES)], axis=-1).astype(bf16)
    wg = w[:, n_shift + n_ml:].astype(bf16)

    x = jnp.concatenate([x_prompt.reshape(Np, D), x_sample.reshape(Ns, D)], axis=0)
    tm = _pow2_chunk(Np + Ns, 512)
    h = rmsnorm(x, norm_mix[0], bf16, tm=tm)
    za = matmul(h, wa, _pick_tile(wa.shape[1]), tm=tm)
    zb = matmul(h, wb, _pick_tile(wb.shape[1]), tm=tm)
    zg = matmul(h, wg, _pick_tile(wg.shape[1]), tm=tm)

    row = lambda t: t.reshape(1, -1)
    rwkv_params = (row(pad_shift_cols(mu_shift[0])), row(w0[0]), row(a0[0]), row(k_k[0]), row(k_a[0]), row(r_k[0]),
                   row(lnx_w[0]), row(lnx_b[0]), jnp.pad(w_up[0], ((0, WP - LW), (0, 0))),
                   jnp.pad(a_up[0], ((0, AP - LA), (0, 0))), g_up[0])
    NA = wa.shape[1]
    ya_p, hbd_p = rwkv(za, 0, Bp, Tp, _pow2_chunk(Tp, RWKV_CHUNK), jnp.zeros((Bp, 1, NA), f32),
                       jnp.zeros((Bp, DA // LANES, LANES, LANES), f32), *rwkv_params)
    ya_s, hbd_s = rwkv(za, Np, Bs, Ts, _pow2_chunk(Ts, RWKV_CHUNK), pad_shift_cols(state_shift[0])[:, None, :],
                       _state_to_blockdiag(state_rwkv[0]), *rwkv_params)

    bias_i = jnp.pad(b_igate[0], (0, LANES - HB)).reshape(1, LANES)
    bias_f = jnp.pad(b_fgate[0], (HB, LANES - 2 * HB)).reshape(1, LANES)
    mh_w = row(mh_norm[0])
    yb_p, c_p, n_p, m_p = mlstm(zb, 0, Bp, Tp, math.gcd(Tp, MLSTM_CHUNK), bias_i, bias_f, mh_w,
                                jnp.zeros((Bp, HB, DK, DV), f32), jnp.zeros((Bp, HB, DK), f32), jnp.zeros((Bp, HB), f32))
    yb_s, c_s, n_s, m_s = mlstm(zb, Np, Bs, Ts, math.gcd(Ts, MLSTM_CHUNK), bias_i, bias_f, mh_w,
                                state_mlstm_c[0], state_mlstm_n[0], state_mlstm_m[0])

    ya = jnp.concatenate([ya_p, ya_s], axis=0)
    yb = jnp.concatenate([yb_p, yb_s], axis=0)
    tn = _pick_tile(D, 512)
    u = merge(ya, yb, p_a[0].astype(bf16), p_b[0].astype(bf16), zg, row(b_gate[0]), tm=tm, tn=tn)
    x1 = outproj(u, w_out[0].astype(bf16), x, tm=tm, tn=tn)
    y = moe_and_final_norm(x1, norm_ffn[0], w_router[0], b_router[0], w_mlp1[0], b_mlp1[0], w_mlp2[0], b_mlp2[0],
                           norm_final, tb=256, tf=_pick_tile(w_mlp2.shape[2], 512), td=tn,
                           tm_router=min(tm, 256), tm_combine=min(tm, 128))

    shift_p = unpad_shift_cols(za[:Np].reshape(Bp, Tp, NA)[:, -1])
    shift_s = unpad_shift_cols(za[Np:].reshape(Bs, Ts, NA)[:, -1])
    return (y[:Np].reshape(Bp, Tp, D), y[Np:].reshape(Bs, Ts, D),
            shift_p[None], _blockdiag_to_state(hbd_p)[None], c_p[None], n_p[None], m_p.reshape(1, Bp, HB),
            shift_s[None], _blockdiag_to_state(hbd_s)[None], c_s[None], n_s[None], m_s.reshape(1, Bs, HB))
```

```python
import functools
import math

import jax
import jax.numpy as jnp
from jax import lax
from jax.experimental import pallas as pl
from jax.experimental.pallas import tpu as pltpu

f32 = jnp.float32
bf16 = jnp.bfloat16
i32 = jnp.int32

LANES = 128
HEAD_A = 64
NORM_EPS = 1e-5
GN_EPS = 64e-5
MH_EPS = 1e-6
GATE_CAP = 15.0
TOP_K = 4
SWIGLU_LIMIT = 7.0
SWIGLU_ALPHA = 1.702
RWKV_CHUNK = 64
MLSTM_CHUNK = 128
VMEM_LIMIT = 56 * 1024 * 1024

NN = (((1,), (0,)), ((), ()))
NT = (((1,), (1,)), ((), ()))
TN = (((0,), (0,)), ((), ()))


def _round_up(x, m):
    return (x + m - 1) // m * m


def _mx(x):
    if x.dtype == bf16 or x.shape[0] % 16 != 0:
        return x
    return x.astype(bf16)


def _dot(a, b, dims=NN):
    return lax.dot_general(_mx(a), _mx(b), dims, preferred_element_type=f32)


def _split(x):
    hi = x.astype(bf16)
    lo = (x - hi.astype(f32)).astype(bf16)
    return hi, lo


def _dot3(a, b, dims=NN):
    ah, al = _split(a)
    bh, bl = _split(b)
    d = lambda x, y: lax.dot_general(x, y, dims, preferred_element_type=f32)
    return d(ah, bh) + d(ah, bl) + d(al, bh)


def _dot_hi(a, b, dims=NN):
    return lax.dot_general(a, b, dims, preferred_element_type=f32, precision=lax.Precision.HIGHEST)


def _iota(shape, dim):
    return lax.broadcasted_iota(i32, shape, dim)


def _params(*sem):
    return pltpu.CompilerParams(dimension_semantics=sem, vmem_limit_bytes=VMEM_LIMIT)


def _two_source_specs(tm, d, na, grid_rank):
    if grid_rank == 1:
        return [pl.BlockSpec((tm, d), lambda i: (jnp.minimum(i, na - 1), 0)),
                pl.BlockSpec((tm, d), lambda i: (jnp.maximum(i - na, 0), 0))]
    return [pl.BlockSpec((tm, d), lambda j, i: (jnp.minimum(i, na - 1), j)),
            pl.BlockSpec((tm, d), lambda j, i: (jnp.maximum(i - na, 0), j))]


def _rmsnorm_kernel(xa_ref, xb_ref, g_ref, o_ref, *, NA_BLOCKS):
    def body(x_ref):
        x = x_ref[...]
        y = x * lax.rsqrt(jnp.mean(x * x, axis=-1, keepdims=True) + NORM_EPS)
        o_ref[...] = (y * g_ref[...]).astype(o_ref.dtype)

    pl.when(pl.program_id(0) < NA_BLOCKS)(lambda: body(xa_ref))
    pl.when(pl.program_id(0) >= NA_BLOCKS)(lambda: body(xb_ref))


def rmsnorm(xa, xb, g, out_dtype, tm=512):
    (na_rows, d), nb_rows = xa.shape, xb.shape[0]
    assert na_rows % tm == 0 and nb_rows % tm == 0
    na = na_rows // tm
    return pl.pallas_call(
        functools.partial(_rmsnorm_kernel, NA_BLOCKS=na),
        grid=((na_rows + nb_rows) // tm,),
        in_specs=_two_source_specs(tm, d, na, 1) + [pl.BlockSpec((1, d), lambda i: (0, 0))],
        out_specs=pl.BlockSpec((tm, d), lambda i: (i, 0)),
        out_shape=jax.ShapeDtypeStruct((na_rows + nb_rows, d), out_dtype),
        compiler_params=_params("arbitrary"),
        name="rmsnorm",
    )(xa, xb, g.reshape(1, d))


def _matmul_kernel(x_ref, w_ref, o_ref):
    o_ref[...] = jnp.dot(x_ref[...], w_ref[...], preferred_element_type=f32).astype(o_ref.dtype)


def matmul(x, w, tn, tm=512, out_dtype=f32):
    n, k = x.shape
    m = w.shape[1]
    return pl.pallas_call(
        _matmul_kernel,
        grid=(m // tn, n // tm),
        in_specs=[pl.BlockSpec((tm, k), lambda j, i: (i, 0)), pl.BlockSpec((k, tn), lambda j, i: (0, j))],
        out_specs=pl.BlockSpec((tm, tn), lambda j, i: (i, j)),
        out_shape=jax.ShapeDtypeStruct((n, m), out_dtype),
        compiler_params=_params("parallel", "parallel"),
        name="matmul",
    )(x, w)


def _cap(t):
    return GATE_CAP * jnp.tanh(t / GATE_CAP)


def _log_sigmoid(x):
    return jnp.minimum(x, 0.0) - jnp.log1p(jnp.exp(-jnp.abs(x)))


def _mlstm_kernel(q_ref, k_ref, v_ref, o_ref, g_ref, bi_ref, bf_ref, mhw_ref, c0_ref, n0_ref, m0_ref,
                  y_ref, c_ref, n_ref, m_ref, *, L, H, DK, DV):
    @pl.when(pl.program_id(1) == 0)
    def _():
        c_ref[...] = c0_ref[...]
        n_ref[...] = n0_ref[...]
        m_ref[...] = m0_ref[...]

    gates = g_ref[...]
    li_all = _cap(gates + bi_ref[...])
    lf_all = _log_sigmoid(_cap(gates + bf_ref[...]))
    causal = _iota((L, L), 1) <= _iota((L, L), 0)
    b_all = _dot_hi(causal.astype(f32), lf_all)
    sel = (_iota((8, LANES), 0) == _iota((8, LANES), 1)).astype(f32)
    li_rows = _dot_hi(sel, li_all, NT)
    b_rows = _dot_hi(sel, b_all, NT)

    for h in range(H):
        q = q_ref[:, h * DK:(h + 1) * DK]
        k = k_ref[:, h * DK:(h + 1) * DK] * (DK ** -0.5)
        v = v_ref[:, h * DV:(h + 1) * DV]
        bcol = b_all[:, H + h:H + h + 1]
        licol = li_all[:, h:h + 1]
        brow = b_rows[H + h:H + h + 1, :]
        lirow = li_rows[h:h + 1, :]
        m_prev = m_ref[0, :, h:h + 1]
        log_d = jnp.where(causal, bcol - brow + lirow, -jnp.inf)
        m_inter = m_prev + bcol
        m_t = jnp.maximum(m_inter, jnp.max(log_d, axis=-1, keepdims=True))
        s = _dot(q, k, NT) * jnp.exp(log_d - m_t)
        scale = jnp.exp(m_inter - m_t)
        c_prev = c_ref[0, h]
        n_prev = n_ref[0, h:h + 1, :]
        num = _dot(s, v) + scale * _dot(q, c_prev)
        den = jnp.sum(s, axis=-1, keepdims=True) + scale * jnp.sum(q * n_prev, axis=-1, keepdims=True)
        hh = num / jnp.maximum(jnp.abs(den), jnp.exp(-m_t))
        b_end = bcol[L - 1:L, :]
        g_end = b_end - bcol + licol
        m_new = jnp.maximum(m_prev + b_end, jnp.max(g_end, axis=0, keepdims=True))
        wts = jnp.exp(g_end - m_new)
        dec = jnp.exp(m_prev + b_end - m_new)
        kw = k * wts
        c_ref[0, h] = dec * c_prev + _dot(kw, v, TN)
        n_ref[0, h:h + 1, :] = dec * n_prev + jnp.sum(kw, axis=0, keepdims=True)
        m_ref[0, :, h:h + 1] = m_new
        hn = hh * lax.rsqrt(jnp.mean(hh * hh, axis=-1, keepdims=True) + MH_EPS)
        gate_o = jax.nn.sigmoid(o_ref[:, h * DV:(h + 1) * DV])
        y_ref[:, h * DV:(h + 1) * DV] = (hn * mhw_ref[:, h * DV:(h + 1) * DV] * gate_o).astype(y_ref.dtype)


def mlstm(zb, row0, B, T, L, bias_i, bias_f, mh_w, c0, n0, m0):
    _, H, DK, DV = c0.shape
    assert 2 * H <= 8 and T % L == 0 and row0 % L == 0 and DV == 2 * DK
    nc = T // L
    r0 = row0 // L
    rows = lambda b, c: r0 + b * nc + c
    kern = functools.partial(_mlstm_kernel, L=L, H=H, DK=DK, DV=DV)
    qk_w, v_w = H * DK, H * DV
    gate_blk = (2 * qk_w + 2 * v_w) // LANES
    return pl.pallas_call(
        kern,
        grid=(B, nc),
        in_specs=[
            pl.BlockSpec((L, qk_w), lambda b, c: (rows(b, c), 0)),
            pl.BlockSpec((L, qk_w), lambda b, c: (rows(b, c), 1)),
            pl.BlockSpec((L, v_w), lambda b, c: (rows(b, c), 1)),
            pl.BlockSpec((L, v_w), lambda b, c: (rows(b, c), 2)),
            pl.BlockSpec((L, LANES), lambda b, c: (rows(b, c), gate_blk)),
            pl.BlockSpec((1, LANES), lambda b, c: (0, 0)),
            pl.BlockSpec((1, LANES), lambda b, c: (0, 0)),
            pl.BlockSpec((1, v_w), lambda b, c: (0, 0)),
            pl.BlockSpec((1, H, DK, DV), lambda b, c: (b, 0, 0, 0)),
            pl.BlockSpec((1, H, DK), lambda b, c: (b, 0, 0)),
            pl.BlockSpec((1, 1, H), lambda b, c: (b, 0, 0)),
        ],
        out_specs=[
            pl.BlockSpec((L, v_w), lambda b, c: (b * nc + c, 0)),
            pl.BlockSpec((1, H, DK, DV), lambda b, c: (b, 0, 0, 0)),
            pl.BlockSpec((1, H, DK), lambda b, c: (b, 0, 0)),
            pl.BlockSpec((1, 1, H), lambda b, c: (b, 0, 0)),
        ],
        out_shape=[
            jax.ShapeDtypeStruct((B * T, v_w), bf16),
            jax.ShapeDtypeStruct((B, H, DK, DV), f32),
            jax.ShapeDtypeStruct((B, H, DK), f32),
            jax.ShapeDtypeStruct((B, 1, H), f32),
        ],
        compiler_params=_params("parallel", "arbitrary"),
        name="mlstm",
    )(zb, zb, zb, zb, zb, bias_i, bias_f, mh_w, c0, n0, m0.reshape(B, 1, H))


def _softplus(x):
    return jnp.maximum(x, 0.0) + jnp.log1p(jnp.exp(-jnp.abs(x)))


def _dot_sel(x, sel):
    xh, xl = _split(x)
    s = sel.astype(bf16)
    return jnp.dot(xh, s, preferred_element_type=f32) + jnp.dot(xl, s, preferred_element_type=f32)


def _head_blocks(gw, scale):
    return jnp.where(_iota((gw, gw), 0) // HEAD_A == _iota((gw, gw), 1) // HEAD_A, scale, 0.0).astype(f32)


def _d1(a, b, dims=NN):
    return lax.dot_general(a, b, dims, preferred_element_type=f32)


def _d3(a, b, dims=NN):
    return _d1(a[0], b[0], dims) + _d1(a[0], b[1], dims) + _d1(a[1], b[0], dims)


def _rows(x, sl):
    return tuple(t[sl] for t in x)


def _cat(xs):
    return tuple(jnp.concatenate(ts, axis=0) for ts in zip(*xs))


def _rwkv_kernel(r_ref, k_ref, v_ref, l_ref, sp_ref, mu_ref, w0_ref, a0_ref, kk_ref, ka_ref, rk_ref,
                 lw_ref, lb_ref, wup_ref, aup_ref, gup_ref, s0_ref,
                 o_ref, s_ref,
                 last_r, last_k, last_v, last_l, at_sc, bt_sc, kt_sc, rt_sc, bh_sc, kh_sc, v_sc, gam_sc,
                 y_sc, g_sc, bonus_sc, h_sc, *, L, S, U, DA, WP, AP, TLW):
    NP = DA // LANES
    GW = min(2 * LANES, DA)
    G = HEAD_A // L
    zeros_hh = jnp.zeros((HEAD_A, HEAD_A), f32)

    @pl.when(pl.program_id(1) == 0)
    def _():
        last_r[...] = sp_ref[:, 0, 0:DA]
        last_k[...] = sp_ref[:, 0, DA:2 * DA]
        last_v[...] = sp_ref[:, 0, 2 * DA:3 * DA]
        last_l[...] = sp_ref[:, 0, 3 * DA:3 * DA + TLW]

        def load_state(i, carry):
            s, p = i // NP, i % NP
            top = jnp.concatenate([s0_ref[s, 2 * p].T, zeros_hh], axis=1)
            bot = jnp.concatenate([zeros_hh, s0_ref[s, 2 * p + 1].T], axis=1)
            h_sc[s, p] = jnp.concatenate([top, bot], axis=0)
            return carry

        lax.fori_loop(0, S * NP, load_state, 0)

    def shifted(ref, last, mu):
        pieces = []
        for s in range(S):
            cur = ref[s * L:(s + 1) * L, :]
            prev = jnp.where(_iota(cur.shape, 0) == 0, last[s:s + 1, :], pltpu.roll(cur, 1, 0))
            last[s:s + 1, :] = cur[L - 1:L, :]
            pieces.append(cur + mu * (prev - cur))
        return jnp.concatenate(pieces, axis=0)

    r = shifted(r_ref, last_r, mu_ref[:, 0:DA])
    k = shifted(k_ref, last_k, mu_ref[:, DA:2 * DA])
    v = shifted(v_ref, last_v, mu_ref[:, 2 * DA:3 * DA])
    xl = shifted(l_ref, last_l, mu_ref[:, 3 * DA:3 * DA + TLW])
    xw, xa, xg = xl[:, 0:WP], xl[:, WP:WP + AP], xl[:, WP + AP:]

    w_log = -_softplus(-(w0_ref[...] + _dot3(jnp.tanh(xw), wup_ref[...]))) - 0.5
    logw = -jnp.exp(w_log)
    a = jax.nn.sigmoid(a0_ref[...] + _dot3(xa, aup_ref[...]))
    g_sc[...] = _dot(jax.nn.sigmoid(xg), gup_ref[...])

    ones_bd = _head_blocks(GW, 1.0)
    seg_sum = lambda x: jnp.concatenate(
        [_dot_sel(x[:, i * GW:(i + 1) * GW], ones_bd) for i in range(DA // GW)], axis=1)
    kk = k * kk_ref[...]
    kk = kk / jnp.maximum(jnp.sqrt(seg_sum(kk * kk)), 1e-12)
    k_mod = k * (1.0 + (a - 1.0) * ka_ref[...])
    bonus_sc[...] = seg_sum(r * k_mod * rk_ref[...]) * v

    R = S * L
    rr, rc = _iota((R, R), 0), _iota((R, R), 1)
    tri = ((rr // L == rc // L) & (rc <= rr)).astype(f32)
    cs = _dot_hi(tri, logw)
    cs_last = [cs[s * L + L - 1:s * L + L, :] for s in range(S)]
    cs_end = jnp.concatenate([jnp.broadcast_to(t, (L, DA)) for t in cs_last], axis=0)
    e_neg = jnp.exp(-cs)
    e_end = jnp.exp(cs_end - cs)
    bv = kk * a
    vals = (
        (at_sc, -kk * jnp.exp(cs - logw)), (bt_sc, bv * e_neg), (kt_sc, k_mod * e_neg), (rt_sc, r * jnp.exp(cs)),
        (bh_sc, bv * e_end), (kh_sc, k_mod * e_end), (v_sc, v),
    )
    gam = jnp.exp(jnp.concatenate(cs_last, axis=0))
    for p in range(NP):
        sl = slice(p * LANES, (p + 1) * LANES)
        for ref, val in vals:
            ref[p] = val[:, sl]
        gam_sc[p] = gam[:, sl]

    lane = _iota((1, LANES), 1)
    m0 = (lane < HEAD_A).astype(f32)
    m1 = 1.0 - m0
    SR = 2 * G * L
    ri = _iota((SR, SR), 0)
    ci = _iota((SR, SR), 1)
    same_blk = ri // L == ci // L
    strict = same_blk & (ci < ri)
    incl = same_blk & (ci <= ri)
    eye = (ri == ci).astype(f32)
    eye_l = _iota((LANES, LANES), 0) == _iota((LANES, LANES), 1)
    n_double = int(math.log2(L)) - 1

    seq_rows = [slice(j * 2 * L, (j + 1) * 2 * L) for j in range(G)]

    def pair_body(i, carry):
        pairs = [i * U + q for q in range(U)]
        chains = [(p, s0) for p in pairs for s0 in range(0, S, G)]
        each = lambda f, *lists: [f(*t) for t in zip(*lists)]
        hsp = [[_split(h_sc[s0 + j, p]) for j in range(G)] for p, s0 in chains]

        def stack(ref):
            out = []
            for p, s0 in chains:
                x = ref[p]
                parts = []
                for s in range(s0, s0 + G):
                    xs = x[s * L:(s + 1) * L]
                    parts += [xs * m0, xs * m1]
                out.append(_split(jnp.concatenate(parts, axis=0)))
            return out

        la, lr, bt, kt, vs = stack(at_sc), stack(rt_sc), stack(bt_sc), stack(kt_sc), stack(v_sc)
        n_ab = each(lambda a, b: jnp.where(strict, _d3(a, b, NT), 0.0), la, bt)
        a_ak = each(lambda a, k: jnp.where(strict, _d3(a, k, NT), 0.0), la, kt)
        xa = each(lambda a, h: jnp.concatenate([_d3(_rows(a, sl), h[j]) for j, sl in enumerate(seq_rows)], axis=0),
                  la, hsp)
        w = each(lambda x, a, v: x + _d3(_split(a), v), xa, a_ak, vs)
        t_inv = [eye + n for n in n_ab]
        n_pow = [_split(n) for n in n_ab]
        for _ in range(n_double):
            n_pow = [_split(_d3(n, n)) for n in n_pow]
            t_inv = each(lambda t, n: t + _d3(n, _split(t)), t_inv, n_pow)
        u = each(lambda t, ww: _split(_d3(_split(t), _split(ww))), t_inv, w)
        r_b = each(lambda r, b: jnp.where(incl, _d1(r[0], b[0], NT), 0.0).astype(bf16), lr, bt)
        r_k = each(lambda r, k: jnp.where(incl, _d1(r[0], k[0], NT), 0.0).astype(bf16), lr, kt)
        xr = each(lambda r, h: jnp.concatenate([_d1(r[0][sl], h[j][0]) for j, sl in enumerate(seq_rows)], axis=0),
                  lr, hsp)
        y_st = each(lambda x, rb, uu, rk, v: x + _d1(rb, uu[0]) + _d1(rk, v[0]), xr, r_b, u, r_k, vs)
        lbh, lkh = stack(bh_sc), stack(kh_sc)
        h_new = []
        for ci, (p, s0) in enumerate(chains):
            for j, sl in enumerate(seq_rows):
                dg = _split(jnp.where(eye_l, gam_sc[p][s0 + j:s0 + j + 1, :], 0.0))
                lhs = _cat([_rows(lbh[ci], sl), _rows(lkh[ci], sl), dg])
                rhs = _cat([_rows(u[ci], sl), _rows(vs[ci], sl), hsp[ci][j]])
                h_new.append((s0 + j, p, _d3(lhs, rhs, TN)))
        per_pair = S // G
        for q, p in enumerate(pairs):
            ys = []
            for ci in range(q * per_pair, (q + 1) * per_pair):
                ys += [y_st[ci][j * 2 * L:j * 2 * L + L] + y_st[ci][j * 2 * L + L:(j + 1) * 2 * L] for j in range(G)]
            y_sc[p] = jnp.concatenate(ys, axis=0)
        for s, p, h in h_new:
            h_sc[s, p] = h
        return carry

    lax.fori_loop(0, NP // U, pair_body, 0)

    avg_bd = _head_blocks(GW, 1.0 / HEAD_A)
    for i in range(DA // GW):
        sl = slice(i * GW, (i + 1) * GW)
        y = jnp.concatenate([y_sc[i * (GW // LANES) + j] for j in range(GW // LANES)], axis=1)
        d = y - _dot_sel(y, avg_bd)
        yn = d * lax.rsqrt(_dot_sel(d * d, avg_bd) + GN_EPS)
        out = (yn * lw_ref[:, sl] + lb_ref[:, sl] + bonus_sc[:, sl]) * g_sc[:, sl]
        o_ref[:, sl] = out.astype(o_ref.dtype)

    @pl.when(pl.program_id(1) == pl.num_programs(1) - 1)
    def _():
        def store_state(i, carry):
            s, p = i // NP, i % NP
            hbd = h_sc[s, p]
            s_ref[s, 2 * p] = hbd[:HEAD_A, :HEAD_A].T
            s_ref[s, 2 * p + 1] = hbd[HEAD_A:, HEAD_A:].T
            return carry

        lax.fori_loop(0, S * NP, store_state, 0)


def rwkv(za, row0, B, T, L, S, shift_prev, s0, mu, w0, a0, k_k, k_a, r_k, lnx_w, lnx_b, wup, aup, gup, U=2):
    DA = w0.shape[-1]
    WP, AP = wup.shape[0], aup.shape[0]
    TLW = WP + AP + gup.shape[0]
    NA = 3 * DA + TLW
    R = S * L
    G = HEAD_A // L
    NP = DA // LANES
    assert za.shape[1] == NA and (3 * DA) % TLW == 0 and L & (L - 1) == 0 and 8 <= L <= HEAD_A
    assert T % L == 0 and B % S == 0 and S % G == 0 and NP % U == 0 and (S == 1 or T == L) and row0 % R == 0
    nc = T // L
    r0 = row0 // R
    rows = lambda b, c: r0 + b * nc + c
    kern = functools.partial(_rwkv_kernel, L=L, S=S, U=U, DA=DA, WP=WP, AP=AP, TLW=TLW)
    vec = pl.BlockSpec((1, DA), lambda b, c: (0, 0))
    full = lambda arr: pl.BlockSpec(arr.shape, lambda b, c: (0,) * arr.ndim)
    state = pl.BlockSpec((S, 2 * NP, HEAD_A, HEAD_A), lambda b, c: (b, 0, 0, 0), pipeline_mode=pl.Buffered(1))
    pair_sc = pltpu.VMEM((NP, R, LANES), f32)
    return pl.pallas_call(
        kern,
        grid=(B // S, nc),
        in_specs=[
            pl.BlockSpec((R, DA), lambda b, c: (rows(b, c), 0)),
            pl.BlockSpec((R, DA), lambda b, c: (rows(b, c), 1)),
            pl.BlockSpec((R, DA), lambda b, c: (rows(b, c), 2)),
            pl.BlockSpec((R, TLW), lambda b, c: (rows(b, c), 3 * DA // TLW)),
            pl.BlockSpec((S, 1, NA), lambda b, c: (b, 0, 0)),
            pl.BlockSpec((1, NA), lambda b, c: (0, 0)),
            vec, vec, vec, vec, vec, vec, vec,
            full(wup), full(aup), full(gup),
            state,
        ],
        out_specs=[pl.BlockSpec((R, DA), lambda b, c: (b * nc + c, 0)), state],
        out_shape=[jax.ShapeDtypeStruct((B * T, DA), bf16), jax.ShapeDtypeStruct(s0.shape, f32)],
        scratch_shapes=[
            pltpu.VMEM((S, DA), f32), pltpu.VMEM((S, DA), f32), pltpu.VMEM((S, DA), f32), pltpu.VMEM((S, TLW), f32),
            pair_sc, pair_sc, pair_sc, pair_sc, pair_sc, pair_sc, pair_sc, pltpu.VMEM((NP, S, LANES), f32),
            pair_sc, pltpu.VMEM((R, DA), f32), pltpu.VMEM((R, DA), f32),
            pltpu.VMEM((S, NP, LANES, LANES), f32),
        ],
        compiler_params=_params("arbitrary", "arbitrary"),
        name="rwkv",
    )(za, za, za, za, shift_prev, mu, w0, a0, k_k, k_a, r_k, lnx_w, lnx_b, wup, aup, gup, s0)


def _merge_kernel(ya1_ref, ya2_ref, yb1_ref, yb2_ref, pa_ref, pb_ref, ga_ref, gb_ref, ba_ref, bb_ref, o_ref, *,
                  NA_BLOCKS):
    ga = jax.nn.sigmoid(ga_ref[...] + ba_ref[...])
    gb = jax.nn.sigmoid(gb_ref[...] + bb_ref[...])

    def body(ya_ref, yb_ref):
        pa = jnp.dot(ya_ref[...], pa_ref[...], preferred_element_type=f32)
        pb = jnp.dot(yb_ref[...], pb_ref[...], preferred_element_type=f32)
        o_ref[...] = (ga * pa + gb * pb).astype(o_ref.dtype)

    pl.when(pl.program_id(1) < NA_BLOCKS)(lambda: body(ya1_ref, yb1_ref))
    pl.when(pl.program_id(1) >= NA_BLOCKS)(lambda: body(ya2_ref, yb2_ref))


def merge(ya, yb, p_a, p_b, zg, b_gate, tm=512, tn=512):
    n = ya[0].shape[0] + ya[1].shape[0]
    da, db = ya[0].shape[1], yb[0].shape[1]
    d = p_a.shape[1]
    nj = d // tn
    assert ya[0].shape[0] % tm == 0 and ya[1].shape[0] % tm == 0
    na = ya[0].shape[0] // tm
    first = lambda j, i: (jnp.minimum(i, na - 1), 0)
    second = lambda j, i: (jnp.maximum(i - na, 0), 0)
    return pl.pallas_call(
        functools.partial(_merge_kernel, NA_BLOCKS=na),
        grid=(nj, n // tm),
        in_specs=[
            pl.BlockSpec((tm, da), first),
            pl.BlockSpec((tm, da), second),
            pl.BlockSpec((tm, db), first),
            pl.BlockSpec((tm, db), second),
            pl.BlockSpec((da, tn), lambda j, i: (0, j)),
            pl.BlockSpec((db, tn), lambda j, i: (0, j)),
            pl.BlockSpec((tm, tn), lambda j, i: (i, j)),
            pl.BlockSpec((tm, tn), lambda j, i: (i, nj + j)),
            pl.BlockSpec((1, tn), lambda j, i: (0, j)),
            pl.BlockSpec((1, tn), lambda j, i: (0, nj + j)),
        ],
        out_specs=pl.BlockSpec((tm, tn), lambda j, i: (i, j)),
        out_shape=jax.ShapeDtypeStruct((n, d), bf16),
        compiler_params=_params("arbitrary", "arbitrary"),
        name="merge",
    )(ya[0], ya[1], yb[0], yb[1], p_a, p_b, zg, zg, b_gate, b_gate)


def _outproj_kernel(u_ref, w_ref, xa_ref, xb_ref, o_ref, *, NA_BLOCKS):
    acc = jnp.dot(u_ref[...], w_ref[...], preferred_element_type=f32)

    @pl.when(pl.program_id(1) < NA_BLOCKS)
    def _():
        o_ref[...] = xa_ref[...] + acc

    @pl.when(pl.program_id(1) >= NA_BLOCKS)
    def _():
        o_ref[...] = xb_ref[...] + acc


def outproj(u, w, xa, xb, tm=512, tn=512):
    n, k = u.shape
    d = w.shape[1]
    assert xa.shape[0] % tm == 0 and xa.shape[0] + xb.shape[0] == n
    na = xa.shape[0] // tm
    return pl.pallas_call(
        functools.partial(_outproj_kernel, NA_BLOCKS=na),
        grid=(d // tn, n // tm),
        in_specs=[pl.BlockSpec((tm, k), lambda j, i: (i, 0)), pl.BlockSpec((k, tn), lambda j, i: (0, j))]
        + _two_source_specs(tm, tn, na, 2),
        out_specs=pl.BlockSpec((tm, tn), lambda j, i: (i, j)),
        out_shape=jax.ShapeDtypeStruct((n, d), f32),
        compiler_params=_params("arbitrary", "arbitrary"),
        name="outproj",
    )(u, w, xa, xb)


def _router_kernel(x_ref, g_ref, wr_ref, br_ref, h_ref, idx_ref, gate_ref, *, E):
    x = x_ref[...]
    h = x * lax.rsqrt(jnp.mean(x * x, axis=-1, keepdims=True) + NORM_EPS) * g_ref[...]
    bits = lax.bitcast_convert_type(h.astype(bf16).astype(f32), jnp.uint32)
    half = x.shape[1] // 2
    h_ref[...] = (bits[:, :half] & jnp.uint32(0xFFFF0000)) | (bits[:, half:] >> 16)
    logits = _dot_hi(h, wr_ref[...]) + br_ref[...]
    lane = _iota(logits.shape, 1)
    l = jnp.where(lane < E, logits, -jnp.inf)
    vals, idxs = [], []
    for _ in range(TOP_K):
        mx = jnp.max(l, axis=-1, keepdims=True)
        ix = jnp.min(jnp.where(l == mx, lane, LANES), axis=-1, keepdims=True)
        vals.append(mx)
        idxs.append(ix)
        l = jnp.where(lane == ix, -jnp.inf, l)
    es = [jnp.exp(v - vals[0]) for v in vals]
    tot = functools.reduce(lambda a, b: a + b, es)
    gate_out = jnp.zeros(logits.shape, f32)
    idx_out = jnp.zeros(logits.shape, i32)
    for k in range(TOP_K):
        gate_out = jnp.where(lane == k, es[k] / tot, gate_out)
        idx_out = jnp.where(lane == k, idxs[k], idx_out)
    gate_ref[...] = gate_out
    idx_ref[...] = idx_out


def router(x, g, w_router, b_router, tm=256):
    n, d = x.shape
    E = w_router.shape[1]
    wr = jnp.pad(w_router, ((0, 0), (0, LANES - E)))
    br = jnp.pad(b_router, (0, LANES - E)).reshape(1, LANES)
    return pl.pallas_call(
        functools.partial(_router_kernel, E=E),
        grid=(n // tm,),
        in_specs=[
            pl.BlockSpec((tm, d), lambda i: (i, 0)),
            pl.BlockSpec((1, d), lambda i: (0, 0)),
            pl.BlockSpec((d, LANES), lambda i: (0, 0)),
            pl.BlockSpec((1, LANES), lambda i: (0, 0)),
        ],
        out_specs=[
            pl.BlockSpec((tm, d // 2), lambda i: (i, 0)),
            pl.BlockSpec((tm, LANES), lambda i: (i, 0)),
            pl.BlockSpec((tm, LANES), lambda i: (i, 0)),
        ],
        out_shape=[
            jax.ShapeDtypeStruct((n, d // 2), jnp.uint32),
            jax.ShapeDtypeStruct((n, LANES), i32),
            jax.ShapeDtypeStruct((n, LANES), f32),
        ],
        compiler_params=_params("parallel"),
        name="router",
    )(x, g.reshape(1, d), wr, br)


def _route(idx, E, tb):
    n, k = idx.shape
    sel = idx[:, :, None] == jnp.arange(E, dtype=i32)[None, None, :]
    onehot = jnp.sum(sel.astype(i32), axis=1)
    pos = jnp.cumsum(onehot, axis=0) - onehot
    counts = jnp.sum(onehot, axis=0)
    padded = (counts + tb - 1) // tb * tb
    pend = jnp.cumsum(padded)
    pstart = pend - padded
    dest = jnp.sum(jnp.where(sel, (pstart[None, :] + pos)[:, None, :], 0), axis=2).astype(i32)
    nb = n * k // tb + E
    tok = jnp.zeros((nb * tb,), i32).at[dest.reshape(-1)].set(jnp.repeat(jnp.arange(n, dtype=i32), k))
    first_row = jnp.arange(nb, dtype=i32) * tb
    block_e = jnp.minimum(jnp.sum((pend[None, :] <= first_row[:, None]).astype(i32), axis=1), E - 1)
    n_used = (pend[-1] // tb).astype(i32)
    group_end = pend[block_e] // tb
    next_e = jnp.where(group_end < n_used, block_e[jnp.minimum(group_end, nb - 1)], -1).astype(i32)
    return dest, tok, block_e, n_used.reshape(1), next_e


def _gather_kernel(tok_ref, src_hbm, o_ref, buf, sem, *, RB):
    base = pl.program_id(0) * RB

    def issue(r, carry):
        pltpu.make_async_copy(src_hbm.at[pl.ds(tok_ref[base + r], 1)], buf.at[pl.ds(r, 1)], sem).start()
        return carry

    lax.fori_loop(0, RB, issue, 0)
    pltpu.make_async_copy(src_hbm.at[pl.ds(0, RB)], buf, sem).wait()
    packed = buf[...]
    half = packed.shape[1]
    hi = lax.bitcast_convert_type(packed & jnp.uint32(0xFFFF0000), f32)
    lo = lax.bitcast_convert_type(packed << 16, f32)
    o_ref[:, :half] = hi.astype(o_ref.dtype)
    o_ref[:, half:] = lo.astype(o_ref.dtype)


def gather_rows(src, tok, rb=256):
    p = tok.shape[0]
    d = 2 * src.shape[1]
    return pl.pallas_call(
        functools.partial(_gather_kernel, RB=rb),
        grid_spec=pltpu.PrefetchScalarGridSpec(
            num_scalar_prefetch=1,
            grid=(p // rb,),
            in_specs=[pl.BlockSpec(memory_space=pl.ANY)],
            out_specs=pl.BlockSpec((rb, d), lambda i, tok: (i, 0)),
            scratch_shapes=[pltpu.VMEM((rb, d // 2), src.dtype), pltpu.SemaphoreType.DMA(())],
        ),
        out_shape=jax.ShapeDtypeStruct((p, d), bf16),
        compiler_params=_params("arbitrary"),
        name="gather_rows",
    )(tok, src)


def _stream_expert_weights(be_ref, nu_ref, nx_ref, tile_copy, n_tiles, land, w_bf):
    j, i = pl.program_id(0), pl.program_id(1)
    nj = pl.num_programs(0)

    def fetch(e, jj):
        for t in range(n_tiles):
            tile_copy(e, jj, t).start()

    @pl.when((j == 0) & (i == 0))
    def _():
        fetch(be_ref[0], 0)

    first = (i < nu_ref[0]) & ((i == 0) | (be_ref[i] != be_ref[jnp.maximum(i - 1, 0)]))

    @pl.when(first)
    def _():
        for t in range(n_tiles):
            tile_copy(0, 0, t).wait()
            w_bf[t] = land[t].astype(bf16)
        nxt = nx_ref[i]

        @pl.when(nxt >= 0)
        def _():
            fetch(nxt, j)

        @pl.when((nxt < 0) & (j + 1 < nj))
        def _():
            fetch(be_ref[0], j + 1)


def _expert_up_kernel(be_ref, nu_ref, nx_ref, x_ref, w_hbm, bg_ref, bl_ref, o_ref, land, w_bf, sem, *, tf, f):
    def tile_copy(e, jj, t):
        col = pl.multiple_of(t * f + jj * tf, LANES)
        return pltpu.make_async_copy(w_hbm.at[e, :, pl.ds(col, tf)], land.at[t], sem.at[t])

    _stream_expert_weights(be_ref, nu_ref, nx_ref, tile_copy, 2, land, w_bf)
    i = pl.program_id(1)

    @pl.when(i < nu_ref[0])
    def _():
        x = x_ref[...]
        gate = jnp.dot(x, w_bf[0], preferred_element_type=f32) + bg_ref[...]
        lin = jnp.dot(x, w_bf[1], preferred_element_type=f32) + bl_ref[...]
        gate = jnp.minimum(gate, SWIGLU_LIMIT)
        lin = jnp.clip(lin, -SWIGLU_LIMIT, SWIGLU_LIMIT)
        o_ref[...] = (gate * jax.nn.sigmoid(SWIGLU_ALPHA * gate) * (lin + 1.0)).astype(o_ref.dtype)

    @pl.when(i >= nu_ref[0])
    def _():
        o_ref[...] = jnp.zeros(o_ref.shape, o_ref.dtype)


def expert_up(xg, w1, b1, block_e, n_used, next_e, tb, tf=512):
    p, d = xg.shape
    E, _, f2 = w1.shape
    f = f2 // 2
    nb = p // tb
    nj = f // tf
    return pl.pallas_call(
        functools.partial(_expert_up_kernel, tf=tf, f=f),
        grid_spec=pltpu.PrefetchScalarGridSpec(
            num_scalar_prefetch=3,
            grid=(nj, nb),
            in_specs=[
                pl.BlockSpec((tb, d), lambda j, i, be, nu, nx: (i, 0)),
                pl.BlockSpec(memory_space=pl.ANY),
                pl.BlockSpec((None, 1, tf), lambda j, i, be, nu, nx: (be[i], 0, j)),
                pl.BlockSpec((None, 1, tf), lambda j, i, be, nu, nx: (be[i], 0, nj + j)),
            ],
            out_specs=pl.BlockSpec((tb, tf), lambda j, i, be, nu, nx: (i, j)),
            scratch_shapes=[pltpu.VMEM((2, d, tf), f32), pltpu.VMEM((2, d, tf), bf16), pltpu.SemaphoreType.DMA((2,))],
        ),
        out_shape=jax.ShapeDtypeStruct((p, f), bf16),
        compiler_params=_params("arbitrary", "arbitrary"),
        name="expert_up",
    )(block_e, n_used, next_e, xg, w1, b1.reshape(E, 1, f2), b1.reshape(E, 1, f2))


def _expert_down_kernel(be_ref, nu_ref, nx_ref, h_ref, w_hbm, b_ref, o_ref, land, w_bf, sem, *, td):
    def tile_copy(e, jj, t):
        col = pl.multiple_of(jj * td, LANES)
        return pltpu.make_async_copy(w_hbm.at[e, :, pl.ds(col, td)], land.at[t], sem.at[t])

    _stream_expert_weights(be_ref, nu_ref, nx_ref, tile_copy, 1, land, w_bf)
    i = pl.program_id(1)

    @pl.when(i < nu_ref[0])
    def _():
        o_ref[...] = jnp.dot(h_ref[...], w_bf[0], preferred_element_type=f32) + b_ref[...]

    @pl.when(i >= nu_ref[0])
    def _():
        o_ref[...] = jnp.zeros(o_ref.shape, o_ref.dtype)


def expert_down(hid, w2, b2, block_e, n_used, next_e, tb, td=512):
    p, f = hid.shape
    E, _, d = w2.shape
    return pl.pallas_call(
        functools.partial(_expert_down_kernel, td=td),
        grid_spec=pltpu.PrefetchScalarGridSpec(
            num_scalar_prefetch=3,
            grid=(d // td, p // tb),
            in_specs=[
                pl.BlockSpec((tb, f), lambda j, i, be, nu, nx: (i, 0)),
                pl.BlockSpec(memory_space=pl.ANY),
                pl.BlockSpec((None, 1, td), lambda j, i, be, nu, nx: (be[i], 0, j)),
            ],
            out_specs=pl.BlockSpec((tb, td), lambda j, i, be, nu, nx: (i, j)),
            scratch_shapes=[pltpu.VMEM((1, f, td), f32), pltpu.VMEM((1, f, td), bf16), pltpu.SemaphoreType.DMA((1,))],
        ),
        out_shape=jax.ShapeDtypeStruct((p, d), f32),
        compiler_params=_params("arbitrary", "arbitrary"),
        name="expert_down",
    )(block_e, n_used, next_e, hid, w2, b2.reshape(E, 1, d))


def _combine_kernel(dest_ref, x_ref, gate_ref, nf_ref, y_hbm, oa_ref, ob_ref, buf, sem, *, TM, NA_BLOCKS):
    base = pl.program_id(0) * TM * TOP_K

    def issue(t, carry):
        for k in range(TOP_K):
            row = dest_ref[base + t * TOP_K + k]
            pltpu.make_async_copy(y_hbm.at[pl.ds(row, 1)], buf.at[k, pl.ds(t, 1)], sem).start()
        return carry

    lax.fori_loop(0, TM, issue, 0)
    for k in range(TOP_K):
        pltpu.make_async_copy(y_hbm.at[pl.ds(0, TM)], buf.at[k], sem).wait()
    acc = x_ref[...]
    for k in range(TOP_K):
        acc = acc + gate_ref[:, k:k + 1] * buf[k]
    y = acc * lax.rsqrt(jnp.mean(acc * acc, axis=-1, keepdims=True) + NORM_EPS) * nf_ref[...]

    @pl.when(pl.program_id(0) < NA_BLOCKS)
    def _():
        oa_ref[...] = y

    @pl.when(pl.program_id(0) >= NA_BLOCKS)
    def _():
        ob_ref[...] = y


def combine(x1, gates, norm_final, yb, dest, n_first, tm=128):
    n, d = x1.shape
    na = n_first // tm
    assert n_first % tm == 0 and 0 < na < n // tm
    return pl.pallas_call(
        functools.partial(_combine_kernel, TM=tm, NA_BLOCKS=na),
        grid_spec=pltpu.PrefetchScalarGridSpec(
            num_scalar_prefetch=1,
            grid=(n // tm,),
            in_specs=[
                pl.BlockSpec((tm, d), lambda i, dest: (i, 0)),
                pl.BlockSpec((tm, LANES), lambda i, dest: (i, 0)),
                pl.BlockSpec((1, d), lambda i, dest: (0, 0)),
                pl.BlockSpec(memory_space=pl.ANY),
            ],
            out_specs=[
                pl.BlockSpec((tm, d), lambda i, dest: (jnp.minimum(i, na - 1), 0)),
                pl.BlockSpec((tm, d), lambda i, dest: (jnp.maximum(i - na, 0), 0)),
            ],
            scratch_shapes=[pltpu.VMEM((TOP_K, tm, d), f32), pltpu.SemaphoreType.DMA(())],
        ),
        out_shape=[jax.ShapeDtypeStruct((n_first, d), f32), jax.ShapeDtypeStruct((n - n_first, d), f32)],
        compiler_params=_params("arbitrary"),
        name="combine",
    )(dest.reshape(-1), x1, gates, norm_final.reshape(1, d), yb)


def moe_and_final_norm(x1, norm_ffn, w_router, b_router, w1, b1, w2, b2, norm_final, n_first, tb, tf=512, td=512,
                       tm_router=256, tm_combine=128):
    h2, idx, gates = router(x1, norm_ffn, w_router, b_router, tm=tm_router)
    dest, tok, block_e, n_used, next_e = _route(idx[:, :TOP_K], w_router.shape[1], tb)
    xg = gather_rows(h2, tok, rb=tb)
    hid = expert_up(xg, w1, b1, block_e, n_used, next_e, tb, tf=tf)
    yb = expert_down(hid, w2, b2, block_e, n_used, next_e, tb, td=td)
    return combine(x1, gates, norm_final, yb, dest, n_first, tm=tm_combine)


def _pick_tile(m, cap=1024):
    units = m // LANES
    best = max(u for u in range(1, cap // LANES + 1) if units % u == 0)
    return best * LANES


def _pow2_chunk(t, cap):
    c = 1
    while c * 2 <= cap and t % (c * 2) == 0:
        c *= 2
    return c


def _pad_last(x, width):
    return jnp.pad(x, [(0, 0)] * (x.ndim - 1) + [(0, width - x.shape[-1])])


def kernel(x_prompt, x_sample, state_shift, state_rwkv, state_mlstm_c, state_mlstm_n, state_mlstm_m, norm_mix, w_in, mu_shift, w0, w_up, a0, a_up, g_up, k_k, k_a, r_k, lnx_w, lnx_b, b_igate, b_fgate, mh_norm, b_gate, p_a, p_b, w_out, norm_ffn, w_router, b_router, w_mlp1, b_mlp1, w_mlp2, b_mlp2, norm_final):
    assert norm_mix.shape[0] == 1, "single trunk layer"
    Bp, Tp, D = x_prompt.shape
    Bs, Ts, _ = x_sample.shape
    Np, Ns = Bp * Tp, Bs * Ts
    DA, LW, LA, LG = w0.shape[-1], w_up.shape[1], a_up.shape[1], g_up.shape[1]
    HA = r_k.shape[1]
    assert r_k.shape[2] == HEAD_A and HA * HEAD_A == DA and LG % LANES == 0
    _, _, HB, DK, DV = state_mlstm_c.shape
    DQK, DB = HB * DK, HB * DV
    n_shift = 3 * DA + LW + LA + LG
    n_ml = 2 * DQK + 2 * DB + 2 * HB
    WP, AP = _round_up(LW, LANES), _round_up(LA, LANES)

    def pad_shift_cols(t):
        o = 3 * DA
        return jnp.concatenate(
            [t[..., :o], _pad_last(t[..., o:o + LW], WP), _pad_last(t[..., o + LW:o + LW + LA], AP),
             t[..., o + LW + LA:]], axis=-1)

    def unpad_shift_cols(t):
        o = 3 * DA
        return jnp.concatenate([t[..., :o], t[..., o:o + LW], t[..., o + WP:o + WP + LA], t[..., o + WP + AP:]], axis=-1)

    w = w_in[0]
    wa = pad_shift_cols(w[:, :n_shift]).astype(bf16)
    ob = n_shift + 2 * DQK + 2 * DB
    wb = jnp.concatenate([w[:, n_shift:ob], _pad_last(w[:, ob:n_shift + n_ml], LANES)], axis=-1).astype(bf16)
    wg = w[:, n_shift + n_ml:].astype(bf16)

    xp, xs = x_prompt.reshape(Np, D), x_sample.reshape(Ns, D)
    tm = _pow2_chunk(math.gcd(Np, Ns), 512)
    h = rmsnorm(xp, xs, norm_mix[0], bf16, tm=tm)
    za = matmul(h, wa, _pick_tile(wa.shape[1]), tm=tm)
    zb = matmul(h, wb, _pick_tile(wb.shape[1]), tm=tm)
    zg = matmul(h, wg, _pick_tile(wg.shape[1]), tm=tm)

    row = lambda t: t.reshape(1, -1)
    rwkv_params = (row(pad_shift_cols(mu_shift[0])), row(w0[0]), row(a0[0]), row(k_k[0]), row(k_a[0]), row(r_k[0]),
                   row(lnx_w[0]), row(lnx_b[0]), jnp.pad(w_up[0], ((0, WP - LW), (0, 0))),
                   jnp.pad(a_up[0], ((0, AP - LA), (0, 0))), g_up[0])
    NA = wa.shape[1]

    def rwkv_group(row0, B, T, shift_prev, s0):
        L = _pow2_chunk(T, RWKV_CHUNK)
        S, U = (1, 4) if T > L else (HEAD_A // L, 2)
        return rwkv(za, row0, B, T, L, S, shift_prev, s0, *rwkv_params, U=U)

    ya_p, s_p = rwkv_group(0, Bp, Tp, jnp.zeros((Bp, 1, NA), f32), jnp.zeros((Bp, HA, HEAD_A, HEAD_A), f32))
    ya_s, s_s = rwkv_group(Np, Bs, Ts, pad_shift_cols(state_shift[0])[:, None, :], state_rwkv[0])

    bias_i = jnp.pad(b_igate[0], (0, LANES - HB)).reshape(1, LANES)
    bias_f = jnp.pad(b_fgate[0], (HB, LANES - 2 * HB)).reshape(1, LANES)
    mh_w = row(mh_norm[0])
    yb_p, c_p, n_p, m_p = mlstm(zb, 0, Bp, Tp, math.gcd(Tp, MLSTM_CHUNK), bias_i, bias_f, mh_w,
                                jnp.zeros((Bp, HB, DK, DV), f32), jnp.zeros((Bp, HB, DK), f32), jnp.zeros((Bp, HB), f32))
    yb_s, c_s, n_s, m_s = mlstm(zb, Np, Bs, Ts, math.gcd(Ts, MLSTM_CHUNK), bias_i, bias_f, mh_w,
                                state_mlstm_c[0], state_mlstm_n[0], state_mlstm_m[0])

    tn = _pick_tile(D, 512)
    u = merge((ya_p, ya_s), (yb_p, yb_s), p_a[0].astype(bf16), p_b[0].astype(bf16), zg, row(b_gate[0]), tm=tm, tn=tn)
    x1 = outproj(u, w_out[0].astype(bf16), xp, xs, tm=tm, tn=tn)
    y_p, y_s = moe_and_final_norm(x1, norm_ffn[0], w_router[0], b_router[0], w_mlp1[0], b_mlp1[0], w_mlp2[0],
                                  b_mlp2[0], norm_final, Np, tb=256, tf=_pick_tile(w_mlp2.shape[2], 512), td=tn,
                                  tm_router=min(tm, 256), tm_combine=min(tm, 128))

    shift_p = unpad_shift_cols(za[:Np].reshape(Bp, Tp, NA)[:, -1])
    shift_s = unpad_shift_cols(za[Np:].reshape(Bs, Ts, NA)[:, -1])
    return (y_p.reshape(Bp, Tp, D), y_s.reshape(Bs, Ts, D),
            shift_p[None], s_p[None], c_p[None], n_p[None], m_p.reshape(1, Bp, HB),
            shift_s[None], s_s[None], c_s[None], n_s[None], m_s.reshape(1, Bs, HB))
```

```python
import functools
import math

import jax
import jax.numpy as jnp
from jax import lax
from jax.experimental import pallas as pl
from jax.experimental.pallas import tpu as pltpu

f32 = jnp.float32
bf16 = jnp.bfloat16
i32 = jnp.int32

LANES = 128
HEAD_A = 64
NORM_EPS = 1e-5
GN_EPS = 64e-5
MH_EPS = 1e-6
GATE_CAP = 15.0
TOP_K = 4
SWIGLU_LIMIT = 7.0
SWIGLU_ALPHA = 1.702
RWKV_CHUNK = 64
MLSTM_CHUNK = 128
VMEM_LIMIT = 56 * 1024 * 1024

NN = (((1,), (0,)), ((), ()))
NT = (((1,), (1,)), ((), ()))
TN = (((0,), (0,)), ((), ()))


def _round_up(x, m):
    return (x + m - 1) // m * m


def _mx(x):
    if x.dtype == bf16 or x.shape[0] % 16 != 0:
        return x
    return x.astype(bf16)


def _dot(a, b, dims=NN):
    return lax.dot_general(_mx(a), _mx(b), dims, preferred_element_type=f32)


def _split(x):
    hi = x.astype(bf16)
    lo = (x - hi.astype(f32)).astype(bf16)
    return hi, lo


def _dot3(a, b, dims=NN):
    ah, al = _split(a)
    bh, bl = _split(b)
    d = lambda x, y: lax.dot_general(x, y, dims, preferred_element_type=f32)
    return d(ah, bh) + d(ah, bl) + d(al, bh)


def _dot_hi(a, b, dims=NN):
    return lax.dot_general(a, b, dims, preferred_element_type=f32, precision=lax.Precision.HIGHEST)


def _iota(shape, dim):
    return lax.broadcasted_iota(i32, shape, dim)


def _params(*sem):
    return pltpu.CompilerParams(dimension_semantics=sem, vmem_limit_bytes=VMEM_LIMIT)


def _two_source_specs(tm, d, na, grid_rank):
    if grid_rank == 1:
        return [pl.BlockSpec((tm, d), lambda i: (jnp.minimum(i, na - 1), 0)),
                pl.BlockSpec((tm, d), lambda i: (jnp.maximum(i - na, 0), 0))]
    return [pl.BlockSpec((tm, d), lambda j, i: (jnp.minimum(i, na - 1), j)),
            pl.BlockSpec((tm, d), lambda j, i: (jnp.maximum(i - na, 0), j))]


def _rmsnorm_kernel(xa_ref, xb_ref, g_ref, o_ref, *, NA_BLOCKS):
    def body(x_ref):
        x = x_ref[...]
        y = x * lax.rsqrt(jnp.mean(x * x, axis=-1, keepdims=True) + NORM_EPS)
        o_ref[...] = (y * g_ref[...]).astype(o_ref.dtype)

    pl.when(pl.program_id(0) < NA_BLOCKS)(lambda: body(xa_ref))
    pl.when(pl.program_id(0) >= NA_BLOCKS)(lambda: body(xb_ref))


def rmsnorm(xa, xb, g, out_dtype, tm=512):
    (na_rows, d), nb_rows = xa.shape, xb.shape[0]
    assert na_rows % tm == 0 and nb_rows % tm == 0
    na = na_rows // tm
    return pl.pallas_call(
        functools.partial(_rmsnorm_kernel, NA_BLOCKS=na),
        grid=((na_rows + nb_rows) // tm,),
        in_specs=_two_source_specs(tm, d, na, 1) + [pl.BlockSpec((1, d), lambda i: (0, 0))],
        out_specs=pl.BlockSpec((tm, d), lambda i: (i, 0)),
        out_shape=jax.ShapeDtypeStruct((na_rows + nb_rows, d), out_dtype),
        compiler_params=_params("arbitrary"),
        name="rmsnorm",
    )(xa, xb, g.reshape(1, d))


def _matmul_kernel(x_ref, w_ref, o_ref):
    o_ref[...] = jnp.dot(x_ref[...], w_ref[...], preferred_element_type=f32).astype(o_ref.dtype)


def matmul(x, w, tn, tm=512, out_dtype=f32):
    n, k = x.shape
    m = w.shape[1]
    return pl.pallas_call(
        _matmul_kernel,
        grid=(m // tn, n // tm),
        in_specs=[pl.BlockSpec((tm, k), lambda j, i: (i, 0)), pl.BlockSpec((k, tn), lambda j, i: (0, j))],
        out_specs=pl.BlockSpec((tm, tn), lambda j, i: (i, j)),
        out_shape=jax.ShapeDtypeStruct((n, m), out_dtype),
        compiler_params=_params("parallel", "parallel"),
        name="matmul",
    )(x, w)


def _cap(t):
    return GATE_CAP * jnp.tanh(t / GATE_CAP)


def _log_sigmoid(x):
    return jnp.minimum(x, 0.0) - jnp.log1p(jnp.exp(-jnp.abs(x)))


def _mlstm_kernel(q_ref, k_ref, v_ref, o_ref, g_ref, bi_ref, bf_ref, mhw_ref, c0_ref, n0_ref, m0_ref,
                  y_ref, c_ref, n_ref, m_ref, *, L, H, DK, DV):
    @pl.when(pl.program_id(1) == 0)
    def _():
        c_ref[...] = c0_ref[...]
        n_ref[...] = n0_ref[...]
        m_ref[...] = m0_ref[...]

    gates = g_ref[...]
    li_all = _cap(gates + bi_ref[...])
    lf_all = _log_sigmoid(_cap(gates + bf_ref[...]))
    causal = _iota((L, L), 1) <= _iota((L, L), 0)
    b_all = _dot_hi(causal.astype(f32), lf_all)
    sel = (_iota((8, LANES), 0) == _iota((8, LANES), 1)).astype(f32)
    li_rows = _dot_hi(sel, li_all, NT)
    b_rows = _dot_hi(sel, b_all, NT)

    for h in range(H):
        q = q_ref[:, h * DK:(h + 1) * DK]
        k = k_ref[:, h * DK:(h + 1) * DK] * (DK ** -0.5)
        v = v_ref[:, h * DV:(h + 1) * DV]
        bcol = b_all[:, H + h:H + h + 1]
        licol = li_all[:, h:h + 1]
        brow = b_rows[H + h:H + h + 1, :]
        lirow = li_rows[h:h + 1, :]
        m_prev = m_ref[0, :, h:h + 1]
        log_d = jnp.where(causal, bcol - brow + lirow, -jnp.inf)
        m_inter = m_prev + bcol
        m_t = jnp.maximum(m_inter, jnp.max(log_d, axis=-1, keepdims=True))
        s = _dot(q, k, NT) * jnp.exp(log_d - m_t)
        scale = jnp.exp(m_inter - m_t)
        c_prev = c_ref[0, h]
        n_prev = n_ref[0, h:h + 1, :]
        num = _dot(s, v) + scale * _dot(q, c_prev)
        den = jnp.sum(s, axis=-1, keepdims=True) + scale * jnp.sum(q * n_prev, axis=-1, keepdims=True)
        hh = num / jnp.maximum(jnp.abs(den), jnp.exp(-m_t))
        b_end = bcol[L - 1:L, :]
        g_end = b_end - bcol + licol
        m_new = jnp.maximum(m_prev + b_end, jnp.max(g_end, axis=0, keepdims=True))
        wts = jnp.exp(g_end - m_new)
        dec = jnp.exp(m_prev + b_end - m_new)
        kw = k * wts
        c_ref[0, h] = dec * c_prev + _dot(kw, v, TN)
        n_ref[0, h:h + 1, :] = dec * n_prev + jnp.sum(kw, axis=0, keepdims=True)
        m_ref[0, :, h:h + 1] = m_new
        hn = hh * lax.rsqrt(jnp.mean(hh * hh, axis=-1, keepdims=True) + MH_EPS)
        gate_o = jax.nn.sigmoid(o_ref[:, h * DV:(h + 1) * DV])
        y_ref[:, h * DV:(h + 1) * DV] = (hn * mhw_ref[:, h * DV:(h + 1) * DV] * gate_o).astype(y_ref.dtype)


def mlstm(zb, row0, B, T, L, bias_i, bias_f, mh_w, c0, n0, m0):
    _, H, DK, DV = c0.shape
    assert 2 * H <= 8 and T % L == 0 and row0 % L == 0 and DV == 2 * DK
    nc = T // L
    r0 = row0 // L
    rows = lambda b, c: r0 + b * nc + c
    kern = functools.partial(_mlstm_kernel, L=L, H=H, DK=DK, DV=DV)
    qk_w, v_w = H * DK, H * DV
    gate_blk = (2 * qk_w + 2 * v_w) // LANES
    return pl.pallas_call(
        kern,
        grid=(B, nc),
        in_specs=[
            pl.BlockSpec((L, qk_w), lambda b, c: (rows(b, c), 0)),
            pl.BlockSpec((L, qk_w), lambda b, c: (rows(b, c), 1)),
            pl.BlockSpec((L, v_w), lambda b, c: (rows(b, c), 1)),
            pl.BlockSpec((L, v_w), lambda b, c: (rows(b, c), 2)),
            pl.BlockSpec((L, LANES), lambda b, c: (rows(b, c), gate_blk)),
            pl.BlockSpec((1, LANES), lambda b, c: (0, 0)),
            pl.BlockSpec((1, LANES), lambda b, c: (0, 0)),
            pl.BlockSpec((1, v_w), lambda b, c: (0, 0)),
            pl.BlockSpec((1, H, DK, DV), lambda b, c: (b, 0, 0, 0)),
            pl.BlockSpec((1, H, DK), lambda b, c: (b, 0, 0)),
            pl.BlockSpec((1, 1, H), lambda b, c: (b, 0, 0)),
        ],
        out_specs=[
            pl.BlockSpec((L, v_w), lambda b, c: (b * nc + c, 0)),
            pl.BlockSpec((1, H, DK, DV), lambda b, c: (b, 0, 0, 0)),
            pl.BlockSpec((1, H, DK), lambda b, c: (b, 0, 0)),
            pl.BlockSpec((1, 1, H), lambda b, c: (b, 0, 0)),
        ],
        out_shape=[
            jax.ShapeDtypeStruct((B * T, v_w), bf16),
            jax.ShapeDtypeStruct((B, H, DK, DV), f32),
            jax.ShapeDtypeStruct((B, H, DK), f32),
            jax.ShapeDtypeStruct((B, 1, H), f32),
        ],
        compiler_params=_params("parallel", "arbitrary"),
        name="mlstm",
    )(zb, zb, zb, zb, zb, bias_i, bias_f, mh_w, c0, n0, m0.reshape(B, 1, H))


def _softplus(x):
    return jnp.maximum(x, 0.0) + jnp.log1p(jnp.exp(-jnp.abs(x)))


def _dot_sel(x, sel):
    xh, xl = _split(x)
    s = sel.astype(bf16)
    return jnp.dot(xh, s, preferred_element_type=f32) + jnp.dot(xl, s, preferred_element_type=f32)


def _head_blocks(gw, scale):
    return jnp.where(_iota((gw, gw), 0) // HEAD_A == _iota((gw, gw), 1) // HEAD_A, scale, 0.0).astype(f32)


def _d1(a, b, dims=NN):
    return lax.dot_general(a, b, dims, preferred_element_type=f32)


def _d3(a, b, dims=NN):
    return _d1(a[0], b[0], dims) + _d1(a[0], b[1], dims) + _d1(a[1], b[0], dims)


def _rows(x, sl):
    return tuple(t[sl] for t in x)


def _cat(xs):
    return tuple(jnp.concatenate(ts, axis=0) for ts in zip(*xs))


def _rwkv_kernel(r_ref, k_ref, v_ref, l_ref, sp_ref, mu_ref, w0_ref, a0_ref, kk_ref, ka_ref, rk_ref,
                 lw_ref, lb_ref, wup_ref, aup_ref, gup_ref, s0_ref,
                 o_ref, s_ref, sh_ref,
                 last_r, last_k, last_v, last_l, at_sc, bt_sc, kt_sc, rt_sc, bh_sc, kh_sc, v_sc, gam_sc,
                 y_sc, g_sc, bonus_sc, h_sc, *, L, S, U, DA, WP, AP, TLW):
    NP = DA // LANES
    GW = min(2 * LANES, DA)
    G = HEAD_A // L
    lane = _iota((1, LANES), 1)
    m0 = (lane < HEAD_A).astype(f32)
    m1 = 1.0 - m0

    @pl.when(pl.program_id(1) == 0)
    def _():
        last_r[...] = sp_ref[:, 0, 0:DA]
        last_k[...] = sp_ref[:, 0, DA:2 * DA]
        last_v[...] = sp_ref[:, 0, 2 * DA:3 * DA]
        last_l[...] = sp_ref[:, 0, 3 * DA:3 * DA + TLW]

        def load_state(i, carry):
            s, p = i // NP, i % NP
            hc = s0_ref[s, p]
            h_sc[s, p] = jnp.concatenate([hc * m0, hc * m1], axis=0)
            return carry

        lax.fori_loop(0, S * NP, load_state, 0)

    def shifted(ref, last, mu):
        pieces = []
        for s in range(S):
            cur = ref[s * L:(s + 1) * L, :]
            prev = jnp.where(_iota(cur.shape, 0) == 0, last[s:s + 1, :], pltpu.roll(cur, 1, 0))
            last[s:s + 1, :] = cur[L - 1:L, :]
            pieces.append(cur + mu * (prev - cur))
        return jnp.concatenate(pieces, axis=0)

    r = shifted(r_ref, last_r, mu_ref[:, 0:DA])
    k = shifted(k_ref, last_k, mu_ref[:, DA:2 * DA])
    v = shifted(v_ref, last_v, mu_ref[:, 2 * DA:3 * DA])
    xl = shifted(l_ref, last_l, mu_ref[:, 3 * DA:3 * DA + TLW])
    xw, xa, xg = xl[:, 0:WP], xl[:, WP:WP + AP], xl[:, WP + AP:]

    w_log = -_softplus(-(w0_ref[...] + _dot3(jnp.tanh(xw), wup_ref[...]))) - 0.5
    logw = -jnp.exp(w_log)
    a = jax.nn.sigmoid(a0_ref[...] + _dot3(xa, aup_ref[...]))
    g_sc[...] = _dot(jax.nn.sigmoid(xg), gup_ref[...])

    ones_bd = _head_blocks(GW, 1.0)
    seg_sum = lambda x: jnp.concatenate(
        [_dot_sel(x[:, i * GW:(i + 1) * GW], ones_bd) for i in range(DA // GW)], axis=1)
    kk = k * kk_ref[...]
    kk = kk / jnp.maximum(jnp.sqrt(seg_sum(kk * kk)), 1e-12)
    k_mod = k * (1.0 + (a - 1.0) * ka_ref[...])
    bonus_sc[...] = seg_sum(r * k_mod * rk_ref[...]) * v

    R = S * L
    rr, rc = _iota((R, R), 0), _iota((R, R), 1)
    tri = ((rr // L == rc // L) & (rc <= rr)).astype(f32)
    cs = _dot_hi(tri, logw)
    cs_last = [cs[s * L + L - 1:s * L + L, :] for s in range(S)]
    cs_end = jnp.concatenate([jnp.broadcast_to(t, (L, DA)) for t in cs_last], axis=0)
    e_neg = jnp.exp(-cs)
    e_end = jnp.exp(cs_end - cs)
    bv = kk * a
    vals = (
        (at_sc, -kk * jnp.exp(cs - logw)), (bt_sc, bv * e_neg), (kt_sc, k_mod * e_neg), (rt_sc, r * jnp.exp(cs)),
        (bh_sc, bv * e_end), (kh_sc, k_mod * e_end), (v_sc, v),
    )
    gam = jnp.exp(jnp.concatenate(cs_last, axis=0))
    for p in range(NP):
        sl = slice(p * LANES, (p + 1) * LANES)
        for ref, val in vals:
            ref[p] = val[:, sl]
        gam_sc[p] = gam[:, sl]

    SR = 2 * G * L
    ri = _iota((SR, SR), 0)
    ci = _iota((SR, SR), 1)
    same_blk = ri // L == ci // L
    strict = same_blk & (ci < ri)
    incl = same_blk & (ci <= ri)
    eye = (ri == ci).astype(f32)
    eye_l = _iota((LANES, LANES), 0) == _iota((LANES, LANES), 1)
    n_double = int(math.log2(L)) - 1

    seq_rows = [slice(j * 2 * L, (j + 1) * 2 * L) for j in range(G)]

    def pair_body(i, carry):
        pairs = [i * U + q for q in range(U)]
        chains = [(p, s0) for p in pairs for s0 in range(0, S, G)]
        each = lambda f, *lists: [f(*t) for t in zip(*lists)]
        hsp = [[_split(h_sc[s0 + j, p]) for j in range(G)] for p, s0 in chains]

        def stack(ref):
            out = []
            for p, s0 in chains:
                x = ref[p]
                parts = []
                for s in range(s0, s0 + G):
                    xs = x[s * L:(s + 1) * L]
                    parts += [xs * m0, xs * m1]
                out.append(_split(jnp.concatenate(parts, axis=0)))
            return out

        hi = lambda xs: [x[0] for x in xs]
        cast = lambda xs: [x.astype(bf16) for x in xs]
        la, lr, bt, kt = hi(stack(at_sc)), hi(stack(rt_sc)), hi(stack(bt_sc)), hi(stack(kt_sc))
        vs = stack(v_sc)
        n_ab = each(lambda a, b: jnp.where(strict, _d1(a, b, NT), 0.0), la, bt)
        a_ak = each(lambda a, k: jnp.where(strict, _d1(a, k, NT), 0.0), la, kt)
        xa = each(lambda a, h: jnp.concatenate([_d1(a[sl], h[j][0]) for j, sl in enumerate(seq_rows)], axis=0),
                  la, hsp)
        w = each(lambda x, a, v: x + _d1(a, v[0]), xa, cast(a_ak), vs)
        t_inv = [eye + n for n in n_ab]
        n_pow = cast(n_ab)
        for _ in range(n_double):
            n_pow = cast(each(lambda n: _d1(n, n), n_pow))
            t_inv = each(lambda t, n, tb: t + _d1(n, tb), t_inv, n_pow, cast(t_inv))
        u = each(lambda t, ww: _split(_d1(t, ww)), cast(t_inv), cast(w))
        r_b = cast(each(lambda r, b: jnp.where(incl, _d1(r, b, NT), 0.0), lr, bt))
        r_k = cast(each(lambda r, k: jnp.where(incl, _d1(r, k, NT), 0.0), lr, kt))
        xr = each(lambda r, h: jnp.concatenate([_d1(r[sl], h[j][0]) for j, sl in enumerate(seq_rows)], axis=0),
                  lr, hsp)
        y_st = each(lambda x, rb, uu, rk, v: x + _d1(rb, uu[0]) + _d1(rk, v[0]), xr, r_b, u, r_k, vs)
        lbh, lkh = stack(bh_sc), stack(kh_sc)
        h_new = []
        for ci, (p, s0) in enumerate(chains):
            for j, sl in enumerate(seq_rows):
                dg = _split(jnp.where(eye_l, gam_sc[p][s0 + j:s0 + j + 1, :], 0.0))
                lhs = _cat([_rows(lbh[ci], sl), _rows(lkh[ci], sl), dg])
                rhs = _cat([_rows(u[ci], sl), _rows(vs[ci], sl), hsp[ci][j]])
                h_new.append((s0 + j, p, _d3(lhs, rhs, TN)))
        per_pair = S // G
        for q, p in enumerate(pairs):
            ys = []
            for ci in range(q * per_pair, (q + 1) * per_pair):
                ys += [y_st[ci][j * 2 * L:j * 2 * L + L] + y_st[ci][j * 2 * L + L:(j + 1) * 2 * L] for j in range(G)]
            y_sc[p] = jnp.concatenate(ys, axis=0)
        for s, p, h in h_new:
            h_sc[s, p] = h
        return carry

    lax.fori_loop(0, NP // U, pair_body, 0)

    avg_bd = _head_blocks(GW, 1.0 / HEAD_A)
    for i in range(DA // GW):
        sl = slice(i * GW, (i + 1) * GW)
        y = jnp.concatenate([y_sc[i * (GW // LANES) + j] for j in range(GW // LANES)], axis=1)
        d = y - _dot_sel(y, avg_bd)
        yn = d * lax.rsqrt(_dot_sel(d * d, avg_bd) + GN_EPS)
        out = (yn * lw_ref[:, sl] + lb_ref[:, sl] + bonus_sc[:, sl]) * g_sc[:, sl]
        o_ref[:, sl] = out.astype(o_ref.dtype)

    @pl.when(pl.program_id(1) == pl.num_programs(1) - 1)
    def _():
        def store_state(i, carry):
            s, p = i // NP, i % NP
            hbd = h_sc[s, p]
            s_ref[s, p] = hbd[:HEAD_A] + hbd[HEAD_A:]
            return carry

        lax.fori_loop(0, S * NP, store_state, 0)
        sh_ref[:, 0, 0:DA] = last_r[...]
        sh_ref[:, 0, DA:2 * DA] = last_k[...]
        sh_ref[:, 0, 2 * DA:3 * DA] = last_v[...]
        sh_ref[:, 0, 3 * DA:3 * DA + TLW] = last_l[...]


def rwkv(za, row0, B, T, L, S, shift_prev, s0, mu, w0, a0, k_k, k_a, r_k, lnx_w, lnx_b, wup, aup, gup, U=2):
    DA = w0.shape[-1]
    WP, AP = wup.shape[0], aup.shape[0]
    TLW = WP + AP + gup.shape[0]
    NA = 3 * DA + TLW
    R = S * L
    G = HEAD_A // L
    NP = DA // LANES
    assert za.shape[1] == NA and (3 * DA) % TLW == 0 and L & (L - 1) == 0 and 8 <= L <= HEAD_A
    assert T % L == 0 and B % S == 0 and S % G == 0 and NP % U == 0 and (S == 1 or T == L) and row0 % R == 0
    nc = T // L
    r0 = row0 // R
    rows = lambda b, c: r0 + b * nc + c
    kern = functools.partial(_rwkv_kernel, L=L, S=S, U=U, DA=DA, WP=WP, AP=AP, TLW=TLW)
    vec = pl.BlockSpec((1, DA), lambda b, c: (0, 0))
    full = lambda arr: pl.BlockSpec(arr.shape, lambda b, c: (0,) * arr.ndim)
    state = pl.BlockSpec((S, NP, HEAD_A, LANES), lambda b, c: (b, 0, 0, 0))
    pair_sc = pltpu.VMEM((NP, R, LANES), f32)
    return pl.pallas_call(
        kern,
        grid=(B // S, nc),
        in_specs=[
            pl.BlockSpec((R, DA), lambda b, c: (rows(b, c), 0)),
            pl.BlockSpec((R, DA), lambda b, c: (rows(b, c), 1)),
            pl.BlockSpec((R, DA), lambda b, c: (rows(b, c), 2)),
            pl.BlockSpec((R, TLW), lambda b, c: (rows(b, c), 3 * DA // TLW)),
            pl.BlockSpec((S, 1, NA), lambda b, c: (b, 0, 0)),
            pl.BlockSpec((1, NA), lambda b, c: (0, 0)),
            vec, vec, vec, vec, vec, vec, vec,
            full(wup), full(aup), full(gup),
            state,
        ],
        out_specs=[pl.BlockSpec((R, DA), lambda b, c: (b * nc + c, 0)), state,
                   pl.BlockSpec((S, 1, NA), lambda b, c: (b, 0, 0))],
        out_shape=[jax.ShapeDtypeStruct((B * T, DA), bf16), jax.ShapeDtypeStruct(s0.shape, f32),
                   jax.ShapeDtypeStruct((B, 1, NA), f32)],
        scratch_shapes=[
            pltpu.VMEM((S, DA), f32), pltpu.VMEM((S, DA), f32), pltpu.VMEM((S, DA), f32), pltpu.VMEM((S, TLW), f32),
            pair_sc, pair_sc, pair_sc, pair_sc, pair_sc, pair_sc, pair_sc, pltpu.VMEM((NP, S, LANES), f32),
            pair_sc, pltpu.VMEM((R, DA), f32), pltpu.VMEM((R, DA), f32),
            pltpu.VMEM((S, NP, LANES, LANES), f32),
        ],
        compiler_params=_params("arbitrary", "arbitrary"),
        name="rwkv",
    )(za, za, za, za, shift_prev, mu, w0, a0, k_k, k_a, r_k, lnx_w, lnx_b, wup, aup, gup, s0)


def _merge_kernel(ya1_ref, ya2_ref, yb1_ref, yb2_ref, pa_ref, pb_ref, ga_ref, gb_ref, ba_ref, bb_ref, o_ref, *,
                  NA_BLOCKS):
    ga = jax.nn.sigmoid(ga_ref[...] + ba_ref[...])
    gb = jax.nn.sigmoid(gb_ref[...] + bb_ref[...])

    def body(ya_ref, yb_ref):
        pa = jnp.dot(ya_ref[...], pa_ref[...], preferred_element_type=f32)
        pb = jnp.dot(yb_ref[...], pb_ref[...], preferred_element_type=f32)
        o_ref[...] = (ga * pa + gb * pb).astype(o_ref.dtype)

    pl.when(pl.program_id(1) < NA_BLOCKS)(lambda: body(ya1_ref, yb1_ref))
    pl.when(pl.program_id(1) >= NA_BLOCKS)(lambda: body(ya2_ref, yb2_ref))


def merge(ya, yb, p_a, p_b, zg, b_gate, tm=512, tn=512):
    n = ya[0].shape[0] + ya[1].shape[0]
    da, db = ya[0].shape[1], yb[0].shape[1]
    d = p_a.shape[1]
    nj = d // tn
    assert ya[0].shape[0] % tm == 0 and ya[1].shape[0] % tm == 0
    na = ya[0].shape[0] // tm
    first = lambda j, i: (jnp.minimum(i, na - 1), 0)
    second = lambda j, i: (jnp.maximum(i - na, 0), 0)
    return pl.pallas_call(
        functools.partial(_merge_kernel, NA_BLOCKS=na),
        grid=(nj, n // tm),
        in_specs=[
            pl.BlockSpec((tm, da), first),
            pl.BlockSpec((tm, da), second),
            pl.BlockSpec((tm, db), first),
            pl.BlockSpec((tm, db), second),
            pl.BlockSpec((da, tn), lambda j, i: (0, j)),
            pl.BlockSpec((db, tn), lambda j, i: (0, j)),
            pl.BlockSpec((tm, tn), lambda j, i: (i, j)),
            pl.BlockSpec((tm, tn), lambda j, i: (i, nj + j)),
            pl.BlockSpec((1, tn), lambda j, i: (0, j)),
            pl.BlockSpec((1, tn), lambda j, i: (0, nj + j)),
        ],
        out_specs=pl.BlockSpec((tm, tn), lambda j, i: (i, j)),
        out_shape=jax.ShapeDtypeStruct((n, d), bf16),
        compiler_params=_params("arbitrary", "arbitrary"),
        name="merge",
    )(ya[0], ya[1], yb[0], yb[1], p_a, p_b, zg, zg, b_gate, b_gate)


def _outproj_kernel(u_ref, w_ref, xa_ref, xb_ref, o_ref, *, NA_BLOCKS):
    acc = jnp.dot(u_ref[...], w_ref[...], preferred_element_type=f32)

    @pl.when(pl.program_id(1) < NA_BLOCKS)
    def _():
        o_ref[...] = xa_ref[...] + acc

    @pl.when(pl.program_id(1) >= NA_BLOCKS)
    def _():
        o_ref[...] = xb_ref[...] + acc


def outproj(u, w, xa, xb, tm=512, tn=512):
    n, k = u.shape
    d = w.shape[1]
    assert xa.shape[0] % tm == 0 and xa.shape[0] + xb.shape[0] == n
    na = xa.shape[0] // tm
    return pl.pallas_call(
        functools.partial(_outproj_kernel, NA_BLOCKS=na),
        grid=(d // tn, n // tm),
        in_specs=[pl.BlockSpec((tm, k), lambda j, i: (i, 0)), pl.BlockSpec((k, tn), lambda j, i: (0, j))]
        + _two_source_specs(tm, tn, na, 2),
        out_specs=pl.BlockSpec((tm, tn), lambda j, i: (i, j)),
        out_shape=jax.ShapeDtypeStruct((n, d), f32),
        compiler_params=_params("arbitrary", "arbitrary"),
        name="outproj",
    )(u, w, xa, xb)


def _router_kernel(x_ref, g_ref, wr_ref, br_ref, h_ref, idx_ref, gate_ref, *, E):
    x = x_ref[...]
    h = x * lax.rsqrt(jnp.mean(x * x, axis=-1, keepdims=True) + NORM_EPS) * g_ref[...]
    bits = lax.bitcast_convert_type(h.astype(bf16).astype(f32), jnp.uint32)
    half = x.shape[1] // 2
    h_ref[...] = (bits[:, :half] & jnp.uint32(0xFFFF0000)) | (bits[:, half:] >> 16)
    logits = _dot_hi(h, wr_ref[...]) + br_ref[...]
    lane = _iota(logits.shape, 1)
    l = jnp.where(lane < E, logits, -jnp.inf)
    vals, idxs = [], []
    for _ in range(TOP_K):
        mx = jnp.max(l, axis=-1, keepdims=True)
        ix = jnp.min(jnp.where(l == mx, lane, LANES), axis=-1, keepdims=True)
        vals.append(mx)
        idxs.append(ix)
        l = jnp.where(lane == ix, -jnp.inf, l)
    es = [jnp.exp(v - vals[0]) for v in vals]
    tot = functools.reduce(lambda a, b: a + b, es)
    gate_out = jnp.zeros(logits.shape, f32)
    idx_out = jnp.zeros(logits.shape, i32)
    for k in range(TOP_K):
        gate_out = jnp.where(lane == k, es[k] / tot, gate_out)
        idx_out = jnp.where(lane == k, idxs[k], idx_out)
    gate_ref[...] = gate_out
    idx_ref[...] = idx_out


def router(x, g, w_router, b_router, tm=256):
    n, d = x.shape
    E = w_router.shape[1]
    wr = jnp.pad(w_router, ((0, 0), (0, LANES - E)))
    br = jnp.pad(b_router, (0, LANES - E)).reshape(1, LANES)
    return pl.pallas_call(
        functools.partial(_router_kernel, E=E),
        grid=(n // tm,),
        in_specs=[
            pl.BlockSpec((tm, d), lambda i: (i, 0)),
            pl.BlockSpec((1, d), lambda i: (0, 0)),
            pl.BlockSpec((d, LANES), lambda i: (0, 0)),
            pl.BlockSpec((1, LANES), lambda i: (0, 0)),
        ],
        out_specs=[
            pl.BlockSpec((tm, d // 2), lambda i: (i, 0)),
            pl.BlockSpec((tm, LANES), lambda i: (i, 0)),
            pl.BlockSpec((tm, LANES), lambda i: (i, 0)),
        ],
        out_shape=[
            jax.ShapeDtypeStruct((n, d // 2), jnp.uint32),
            jax.ShapeDtypeStruct((n, LANES), i32),
            jax.ShapeDtypeStruct((n, LANES), f32),
        ],
        compiler_params=_params("parallel"),
        name="router",
    )(x, g.reshape(1, d), wr, br)


def _route(idx, E, tb):
    n, k = idx.shape
    sel = idx[:, :, None] == jnp.arange(E, dtype=i32)[None, None, :]
    onehot = jnp.sum(sel.astype(i32), axis=1)
    pos = jnp.cumsum(onehot, axis=0) - onehot
    counts = jnp.sum(onehot, axis=0)
    padded = (counts + tb - 1) // tb * tb
    pend = jnp.cumsum(padded)
    pstart = pend - padded
    dest = jnp.sum(jnp.where(sel, (pstart[None, :] + pos)[:, None, :], 0), axis=2).astype(i32)
    nb = n * k // tb + E
    tok = (jnp.arange(nb * tb, dtype=i32) % n).at[dest.reshape(-1)].set(jnp.repeat(jnp.arange(n, dtype=i32), k))
    first_row = jnp.arange(nb, dtype=i32) * tb
    block_e = jnp.minimum(jnp.sum((pend[None, :] <= first_row[:, None]).astype(i32), axis=1), E - 1)
    n_used = (pend[-1] // tb).astype(i32)
    group_end = pend[block_e] // tb
    next_e = jnp.where(group_end < n_used, block_e[jnp.minimum(group_end, nb - 1)], -1).astype(i32)
    rows_valid = jnp.clip((pstart + counts)[block_e] - first_row, 0, tb)
    rows_valid = jnp.where(first_row < pend[-1], rows_valid, 0).astype(i32)
    return dest, tok, block_e, rows_valid, next_e


def _gather_kernel(tok_ref, nv_ref, src_hbm, o_ref, buf, sem, *, RB):
    i = pl.program_id(0)
    nblk = pl.num_programs(0)

    def issue_block(blk):
        slot = blk % 2

        def issue(r, carry):
            row = tok_ref[blk * RB + r]
            pltpu.make_async_copy(src_hbm.at[pl.ds(row, 1)], buf.at[slot, pl.ds(r, 1)], sem.at[slot]).start()
            return carry

        lax.fori_loop(0, RB, issue, 0)

    @pl.when(i == 0)
    def _():
        issue_block(0)

    @pl.when((i + 1 < nblk) & (nv_ref[jnp.minimum(i + 1, nblk - 1)] > 0))
    def _():
        issue_block(i + 1)

    @pl.when(nv_ref[i] > 0)
    def _():
        slot = i % 2
        pltpu.make_async_copy(src_hbm.at[pl.ds(0, RB)], buf.at[slot], sem.at[slot]).wait()
        packed = buf[slot]
        half = packed.shape[1]
        hi = lax.bitcast_convert_type(packed & jnp.uint32(0xFFFF0000), f32)
        lo = lax.bitcast_convert_type(packed << 16, f32)
        o_ref[:, :half] = hi.astype(o_ref.dtype)
        o_ref[:, half:] = lo.astype(o_ref.dtype)

    @pl.when(nv_ref[i] == 0)
    def _():
        o_ref[...] = jnp.zeros(o_ref.shape, o_ref.dtype)


def gather_rows(src, tok, rows_valid, rb=256):
    p = tok.shape[0]
    d = 2 * src.shape[1]
    return pl.pallas_call(
        functools.partial(_gather_kernel, RB=rb),
        grid_spec=pltpu.PrefetchScalarGridSpec(
            num_scalar_prefetch=2,
            grid=(p // rb,),
            in_specs=[pl.BlockSpec(memory_space=pl.ANY)],
            out_specs=pl.BlockSpec((rb, d), lambda i, tok, nv: (i, 0)),
            scratch_shapes=[pltpu.VMEM((2, rb, d // 2), src.dtype), pltpu.SemaphoreType.DMA((2,))],
        ),
        out_shape=jax.ShapeDtypeStruct((p, d), bf16),
        compiler_params=_params("arbitrary"),
        name="gather_rows",
    )(tok, rows_valid, src)


def _stream_expert_weights(be_ref, nu_ref, nx_ref, tile_copy, n_tiles, land, w_bf):
    j, i = pl.program_id(0), pl.program_id(1)
    nj = pl.num_programs(0)

    def fetch(e, jj):
        for t in range(n_tiles):
            tile_copy(e, jj, t).start()

    @pl.when((j == 0) & (i == 0))
    def _():
        fetch(be_ref[0], 0)

    first = (nu_ref[i] > 0) & ((i == 0) | (be_ref[i] != be_ref[jnp.maximum(i - 1, 0)]))

    @pl.when(first)
    def _():
        for t in range(n_tiles):
            tile_copy(0, 0, t).wait()
            w_bf[t] = land[t].astype(bf16)
        nxt = nx_ref[i]

        @pl.when(nxt >= 0)
        def _():
            fetch(nxt, j)

        @pl.when((nxt < 0) & (j + 1 < nj))
        def _():
            fetch(be_ref[0], j + 1)


def _on_valid_rows(n_valid, tb, o_ref, compute):
    half = tb // 2

    def zero(rows):
        o_ref[rows, :] = jnp.zeros((rows.stop - rows.start, o_ref.shape[1]), o_ref.dtype)

    @pl.when(n_valid > half)
    def _():
        compute(slice(0, tb))

    @pl.when((n_valid > 0) & (n_valid <= half))
    def _():
        compute(slice(0, half))
        zero(slice(half, tb))

    @pl.when(n_valid == 0)
    def _():
        zero(slice(0, tb))


def _expert_up_kernel(be_ref, nu_ref, nx_ref, x_ref, w_hbm, bg_ref, bl_ref, o_ref, land, w_bf, sem, *, tf, f):
    def tile_copy(e, jj, t):
        col = pl.multiple_of(t * f + jj * tf, LANES)
        return pltpu.make_async_copy(w_hbm.at[e, :, pl.ds(col, tf)], land.at[t], sem.at[t])

    _stream_expert_weights(be_ref, nu_ref, nx_ref, tile_copy, 2, land, w_bf)

    def compute(rows):
        x = x_ref[rows, :]
        gate = jnp.dot(x, w_bf[0], preferred_element_type=f32) + bg_ref[...]
        lin = jnp.dot(x, w_bf[1], preferred_element_type=f32) + bl_ref[...]
        gate = jnp.minimum(gate, SWIGLU_LIMIT)
        lin = jnp.clip(lin, -SWIGLU_LIMIT, SWIGLU_LIMIT)
        o_ref[rows, :] = (gate * jax.nn.sigmoid(SWIGLU_ALPHA * gate) * (lin + 1.0)).astype(o_ref.dtype)

    _on_valid_rows(nu_ref[pl.program_id(1)], x_ref.shape[0], o_ref, compute)


def expert_up(xg, w1, b1, block_e, rows_valid, next_e, tb, tf=512):
    p, d = xg.shape
    E, _, f2 = w1.shape
    f = f2 // 2
    nb = p // tb
    nj = f // tf
    return pl.pallas_call(
        functools.partial(_expert_up_kernel, tf=tf, f=f),
        grid_spec=pltpu.PrefetchScalarGridSpec(
            num_scalar_prefetch=3,
            grid=(nj, nb),
            in_specs=[
                pl.BlockSpec((tb, d), lambda j, i, be, nu, nx: (i, 0)),
                pl.BlockSpec(memory_space=pl.ANY),
                pl.BlockSpec((None, 1, tf), lambda j, i, be, nu, nx: (be[i], 0, j)),
                pl.BlockSpec((None, 1, tf), lambda j, i, be, nu, nx: (be[i], 0, nj + j)),
            ],
            out_specs=pl.BlockSpec((tb, tf), lambda j, i, be, nu, nx: (i, j)),
            scratch_shapes=[pltpu.VMEM((2, d, tf), f32), pltpu.VMEM((2, d, tf), bf16), pltpu.SemaphoreType.DMA((2,))],
        ),
        out_shape=jax.ShapeDtypeStruct((p, f), bf16),
        compiler_params=_params("arbitrary", "arbitrary"),
        name="expert_up",
    )(block_e, rows_valid, next_e, xg, w1, b1.reshape(E, 1, f2), b1.reshape(E, 1, f2))


def _expert_down_kernel(be_ref, nu_ref, nx_ref, h_ref, w_hbm, b_ref, o_ref, land, w_bf, sem, *, td):
    def tile_copy(e, jj, t):
        col = pl.multiple_of(jj * td, LANES)
        return pltpu.make_async_copy(w_hbm.at[e, :, pl.ds(col, td)], land.at[t], sem.at[t])

    _stream_expert_weights(be_ref, nu_ref, nx_ref, tile_copy, 1, land, w_bf)

    def compute(rows):
        o_ref[rows, :] = jnp.dot(h_ref[rows, :], w_bf[0], preferred_element_type=f32) + b_ref[...]

    _on_valid_rows(nu_ref[pl.program_id(1)], h_ref.shape[0], o_ref, compute)


def expert_down(hid, w2, b2, block_e, rows_valid, next_e, tb, td=512):
    p, f = hid.shape
    E, _, d = w2.shape
    return pl.pallas_call(
        functools.partial(_expert_down_kernel, td=td),
        grid_spec=pltpu.PrefetchScalarGridSpec(
            num_scalar_prefetch=3,
            grid=(d // td, p // tb),
            in_specs=[
                pl.BlockSpec((tb, f), lambda j, i, be, nu, nx: (i, 0)),
                pl.BlockSpec(memory_space=pl.ANY),
                pl.BlockSpec((None, 1, td), lambda j, i, be, nu, nx: (be[i], 0, j)),
            ],
            out_specs=pl.BlockSpec((tb, td), lambda j, i, be, nu, nx: (i, j)),
            scratch_shapes=[pltpu.VMEM((1, f, td), f32), pltpu.VMEM((1, f, td), bf16), pltpu.SemaphoreType.DMA((1,))],
        ),
        out_shape=jax.ShapeDtypeStruct((p, d), f32),
        compiler_params=_params("arbitrary", "arbitrary"),
        name="expert_down",
    )(block_e, rows_valid, next_e, hid, w2, b2.reshape(E, 1, d))


def _combine_kernel(dest_ref, x_ref, gate_ref, nf_ref, y_hbm, oa_ref, ob_ref, buf, sem, *, TM, NA_BLOCKS):
    base = pl.program_id(0) * TM * TOP_K

    def issue(t, carry):
        for k in range(TOP_K):
            row = dest_ref[base + t * TOP_K + k]
            pltpu.make_async_copy(y_hbm.at[pl.ds(row, 1)], buf.at[k, pl.ds(t, 1)], sem).start()
        return carry

    lax.fori_loop(0, TM, issue, 0)
    for k in range(TOP_K):
        pltpu.make_async_copy(y_hbm.at[pl.ds(0, TM)], buf.at[k], sem).wait()
    acc = x_ref[...]
    for k in range(TOP_K):
        acc = acc + gate_ref[:, k:k + 1] * buf[k]
    y = acc * lax.rsqrt(jnp.mean(acc * acc, axis=-1, keepdims=True) + NORM_EPS) * nf_ref[...]

    @pl.when(pl.program_id(0) < NA_BLOCKS)
    def _():
        oa_ref[...] = y

    @pl.when(pl.program_id(0) >= NA_BLOCKS)
    def _():
        ob_ref[...] = y


def combine(x1, gates, norm_final, yb, dest, n_first, tm=128):
    n, d = x1.shape
    na = n_first // tm
    assert n_first % tm == 0 and 0 < na < n // tm
    return pl.pallas_call(
        functools.partial(_combine_kernel, TM=tm, NA_BLOCKS=na),
        grid_spec=pltpu.PrefetchScalarGridSpec(
            num_scalar_prefetch=1,
            grid=(n // tm,),
            in_specs=[
                pl.BlockSpec((tm, d), lambda i, dest: (i, 0)),
                pl.BlockSpec((tm, LANES), lambda i, dest: (i, 0)),
                pl.BlockSpec((1, d), lambda i, dest: (0, 0)),
                pl.BlockSpec(memory_space=pl.ANY),
            ],
            out_specs=[
                pl.BlockSpec((tm, d), lambda i, dest: (jnp.minimum(i, na - 1), 0)),
                pl.BlockSpec((tm, d), lambda i, dest: (jnp.maximum(i - na, 0), 0)),
            ],
            scratch_shapes=[pltpu.VMEM((TOP_K, tm, d), f32), pltpu.SemaphoreType.DMA(())],
        ),
        out_shape=[jax.ShapeDtypeStruct((n_first, d), f32), jax.ShapeDtypeStruct((n - n_first, d), f32)],
        compiler_params=_params("arbitrary"),
        name="combine",
    )(dest.reshape(-1), x1, gates, norm_final.reshape(1, d), yb)


def moe_and_final_norm(x1, norm_ffn, w_router, b_router, w1, b1, w2, b2, norm_final, n_first, tb, tf=512, td=512,
                       tm_router=256, tm_combine=128):
    h2, idx, gates = router(x1, norm_ffn, w_router, b_router, tm=tm_router)
    dest, tok, block_e, rows_valid, next_e = _route(idx[:, :TOP_K], w_router.shape[1], tb)
    xg = gather_rows(h2, tok, rows_valid, rb=tb)
    hid = expert_up(xg, w1, b1, block_e, rows_valid, next_e, tb, tf=tf)
    yb = expert_down(hid, w2, b2, block_e, rows_valid, next_e, tb, td=td)
    return combine(x1, gates, norm_final, yb, dest, n_first, tm=tm_combine)


def _pick_tile(m, cap=1024):
    units = m // LANES
    best = max(u for u in range(1, cap // LANES + 1) if units % u == 0)
    return best * LANES


def _pow2_chunk(t, cap):
    c = 1
    while c * 2 <= cap and t % (c * 2) == 0:
        c *= 2
    return c


def _state_to_pairs(s):
    B, H = s.shape[:2]
    t = jnp.swapaxes(s, 2, 3).reshape(B, H // 2, 2, HEAD_A, HEAD_A)
    return jnp.swapaxes(t, 2, 3).reshape(B, H // 2, HEAD_A, 2 * HEAD_A)


def _pairs_to_state(hc):
    B, NP = hc.shape[:2]
    t = jnp.swapaxes(hc.reshape(B, NP, HEAD_A, 2, HEAD_A), 2, 3)
    return jnp.swapaxes(t.reshape(B, 2 * NP, HEAD_A, HEAD_A), 2, 3)


def _pad_last(x, width):
    return jnp.pad(x, [(0, 0)] * (x.ndim - 1) + [(0, width - x.shape[-1])])


def kernel(x_prompt, x_sample, state_shift, state_rwkv, state_mlstm_c, state_mlstm_n, state_mlstm_m, norm_mix, w_in, mu_shift, w0, w_up, a0, a_up, g_up, k_k, k_a, r_k, lnx_w, lnx_b, b_igate, b_fgate, mh_norm, b_gate, p_a, p_b, w_out, norm_ffn, w_router, b_router, w_mlp1, b_mlp1, w_mlp2, b_mlp2, norm_final):
    assert norm_mix.shape[0] == 1, "single trunk layer"
    Bp, Tp, D = x_prompt.shape
    Bs, Ts, _ = x_sample.shape
    Np, Ns = Bp * Tp, Bs * Ts
    DA, LW, LA, LG = w0.shape[-1], w_up.shape[1], a_up.shape[1], g_up.shape[1]
    HA = r_k.shape[1]
    assert r_k.shape[2] == HEAD_A and HA * HEAD_A == DA and LG % LANES == 0
    _, _, HB, DK, DV = state_mlstm_c.shape
    DQK, DB = HB * DK, HB * DV
    n_shift = 3 * DA + LW + LA + LG
    n_ml = 2 * DQK + 2 * DB + 2 * HB
    WP, AP = _round_up(LW, LANES), _round_up(LA, LANES)

    def pad_shift_cols(t):
        o = 3 * DA
        return jnp.concatenate(
            [t[..., :o], _pad_last(t[..., o:o + LW], WP), _pad_last(t[..., o + LW:o + LW + LA], AP),
             t[..., o + LW + LA:]], axis=-1)

    def unpad_shift_cols(t):
        o = 3 * DA
        return jnp.concatenate([t[..., :o], t[..., o:o + LW], t[..., o + WP:o + WP + LA], t[..., o + WP + AP:]], axis=-1)

    w = w_in[0]
    wa = pad_shift_cols(w[:, :n_shift]).astype(bf16)
    ob = n_shift + 2 * DQK + 2 * DB
    wb = jnp.concatenate([w[:, n_shift:ob], _pad_last(w[:, ob:n_shift + n_ml], LANES)], axis=-1).astype(bf16)
    wg = w[:, n_shift + n_ml:].astype(bf16)

    xp, xs = x_prompt.reshape(Np, D), x_sample.reshape(Ns, D)
    tm = _pow2_chunk(math.gcd(Np, Ns), 512)
    h = rmsnorm(xp, xs, norm_mix[0], bf16, tm=tm)
    za = matmul(h, wa, _pick_tile(wa.shape[1]), tm=tm)
    zb = matmul(h, wb, _pick_tile(wb.shape[1]), tm=tm)
    zg = matmul(h, wg, _pick_tile(wg.shape[1]), tm=tm)

    row = lambda t: t.reshape(1, -1)
    rwkv_params = (row(pad_shift_cols(mu_shift[0])), row(w0[0]), row(a0[0]), row(k_k[0]), row(k_a[0]), row(r_k[0]),
                   row(lnx_w[0]), row(lnx_b[0]), jnp.pad(w_up[0], ((0, WP - LW), (0, 0))),
                   jnp.pad(a_up[0], ((0, AP - LA), (0, 0))), g_up[0])
    NA = wa.shape[1]

    def rwkv_group(row0, B, T, shift_prev, s0):
        L = _pow2_chunk(T, RWKV_CHUNK)
        S, U = (1, 4) if T > L else (HEAD_A // L, 2)
        return rwkv(za, row0, B, T, L, S, shift_prev, s0, *rwkv_params, U=U)

    ya_p, s_p, sh_p = rwkv_group(0, Bp, Tp, jnp.zeros((Bp, 1, NA), f32), jnp.zeros((Bp, HA // 2, HEAD_A, LANES), f32))
    ya_s, s_s, sh_s = rwkv_group(Np, Bs, Ts, pad_shift_cols(state_shift[0])[:, None, :],
                                 _state_to_pairs(state_rwkv[0]))
    s_p, s_s = _pairs_to_state(s_p), _pairs_to_state(s_s)

    bias_i = jnp.pad(b_igate[0], (0, LANES - HB)).reshape(1, LANES)
    bias_f = jnp.pad(b_fgate[0], (HB, LANES - 2 * HB)).reshape(1, LANES)
    mh_w = row(mh_norm[0])
    yb_p, c_p, n_p, m_p = mlstm(zb, 0, Bp, Tp, math.gcd(Tp, MLSTM_CHUNK), bias_i, bias_f, mh_w,
                                jnp.zeros((Bp, HB, DK, DV), f32), jnp.zeros((Bp, HB, DK), f32), jnp.zeros((Bp, HB), f32))
    yb_s, c_s, n_s, m_s = mlstm(zb, Np, Bs, Ts, math.gcd(Ts, MLSTM_CHUNK), bias_i, bias_f, mh_w,
                                state_mlstm_c[0], state_mlstm_n[0], state_mlstm_m[0])

    tn = _pick_tile(D, 512)
    u = merge((ya_p, ya_s), (yb_p, yb_s), p_a[0].astype(bf16), p_b[0].astype(bf16), zg, row(b_gate[0]), tm=tm, tn=tn)
    x1 = outproj(u, w_out[0].astype(bf16), xp, xs, tm=tm, tn=tn)
    y_p, y_s = moe_and_final_norm(x1, norm_ffn[0], w_router[0], b_router[0], w_mlp1[0], b_mlp1[0], w_mlp2[0],
                                  b_mlp2[0], norm_final, Np, tb=256, tf=_pick_tile(w_mlp2.shape[2], 512), td=tn,
                                  tm_router=min(tm, 256), tm_combine=min(tm, 128))

    shift_p = unpad_shift_cols(sh_p[:, 0])
    shift_s = unpad_shift_cols(sh_s[:, 0])
    return (y_p.reshape(Bp, Tp, D), y_s.reshape(Bs, Ts, D),
            shift_p[None], s_p[None], c_p[None], n_p[None], m_p.reshape(1, Bp, HB),
            shift_s[None], s_s[None], c_s[None], n_s[None], m_s.reshape(1, Bs, HB))
```

```python
import functools
import math

import jax
import jax.numpy as jnp
from jax import lax
from jax.experimental import pallas as pl
from jax.experimental.pallas import tpu as pltpu

f32 = jnp.float32
bf16 = jnp.bfloat16
i32 = jnp.int32

LANES = 128
HEAD_A = 64
NORM_EPS = 1e-5
GN_EPS = 64e-5
MH_EPS = 1e-6
GATE_CAP = 15.0
TOP_K = 4
SWIGLU_LIMIT = 7.0
SWIGLU_ALPHA = 1.702
RWKV_CHUNK = 64
MLSTM_CHUNK = 128
EXPERT_ROWS = 512
CAST_CHUNK = 512
VMEM_LIMIT = 56 * 1024 * 1024

NN = (((1,), (0,)), ((), ()))
NT = (((1,), (1,)), ((), ()))
TN = (((0,), (0,)), ((), ()))


def _round_up(x, m):
    return (x + m - 1) // m * m


def _mx(x):
    if x.dtype == bf16 or x.shape[0] % 16 != 0:
        return x
    return x.astype(bf16)


def _dot(a, b, dims=NN):
    return lax.dot_general(_mx(a), _mx(b), dims, preferred_element_type=f32)


def _split(x):
    hi = x.astype(bf16)
    lo = (x - hi.astype(f32)).astype(bf16)
    return hi, lo


def _dot3(a, b, dims=NN):
    ah, al = _split(a)
    bh, bl = _split(b)
    d = lambda x, y: lax.dot_general(x, y, dims, preferred_element_type=f32)
    return d(ah, bh) + d(ah, bl) + d(al, bh)


def _dot_hi(a, b, dims=NN):
    return lax.dot_general(a, b, dims, preferred_element_type=f32, precision=lax.Precision.HIGHEST)


def _pack_bf16_halves(x):
    bits = lax.bitcast_convert_type(x.astype(bf16).astype(f32), jnp.uint32)
    half = x.shape[1] // 2
    return (bits[:, :half] & jnp.uint32(0xFFFF0000)) | (bits[:, half:] >> 16)


def _unpack_bf16_halves(packed):
    hi = lax.bitcast_convert_type(packed & jnp.uint32(0xFFFF0000), f32)
    lo = lax.bitcast_convert_type(packed << 16, f32)
    return hi, lo


def _iota(shape, dim):
    return lax.broadcasted_iota(i32, shape, dim)


def _params(*sem):
    return pltpu.CompilerParams(dimension_semantics=sem, vmem_limit_bytes=VMEM_LIMIT)


def _two_source_specs(tm, d, na, grid_rank):
    if grid_rank == 1:
        return [pl.BlockSpec((tm, d), lambda i: (jnp.minimum(i, na - 1), 0)),
                pl.BlockSpec((tm, d), lambda i: (jnp.maximum(i - na, 0), 0))]
    return [pl.BlockSpec((tm, d), lambda j, i: (jnp.minimum(i, na - 1), j)),
            pl.BlockSpec((tm, d), lambda j, i: (jnp.maximum(i - na, 0), j))]


def _rmsnorm_kernel(xa_ref, xb_ref, g_ref, o_ref, *, NA_BLOCKS):
    def body(x_ref):
        x = x_ref[...]
        y = x * lax.rsqrt(jnp.mean(x * x, axis=-1, keepdims=True) + NORM_EPS)
        o_ref[...] = (y * g_ref[...]).astype(o_ref.dtype)

    pl.when(pl.program_id(0) < NA_BLOCKS)(lambda: body(xa_ref))
    pl.when(pl.program_id(0) >= NA_BLOCKS)(lambda: body(xb_ref))


def rmsnorm(xa, xb, g, out_dtype, tm=512):
    (na_rows, d), nb_rows = xa.shape, xb.shape[0]
    assert na_rows % tm == 0 and nb_rows % tm == 0
    na = na_rows // tm
    return pl.pallas_call(
        functools.partial(_rmsnorm_kernel, NA_BLOCKS=na),
        grid=((na_rows + nb_rows) // tm,),
        in_specs=_two_source_specs(tm, d, na, 1) + [pl.BlockSpec((1, d), lambda i: (0, 0))],
        out_specs=pl.BlockSpec((tm, d), lambda i: (i, 0)),
        out_shape=jax.ShapeDtypeStruct((na_rows + nb_rows, d), out_dtype),
        compiler_params=_params("arbitrary"),
        name="rmsnorm",
    )(xa, xb, g.reshape(1, d))


def _matmul_kernel(x_ref, w_ref, o_ref):
    o_ref[...] = jnp.dot(x_ref[...], w_ref[...], preferred_element_type=f32).astype(o_ref.dtype)


def matmul(x, w, tn, tm=512, out_dtype=f32):
    n, k = x.shape
    m = w.shape[1]
    return pl.pallas_call(
        _matmul_kernel,
        grid=(m // tn, n // tm),
        in_specs=[pl.BlockSpec((tm, k), lambda j, i: (i, 0)), pl.BlockSpec((k, tn), lambda j, i: (0, j))],
        out_specs=pl.BlockSpec((tm, tn), lambda j, i: (i, j)),
        out_shape=jax.ShapeDtypeStruct((n, m), out_dtype),
        compiler_params=_params("parallel", "parallel"),
        name="matmul",
    )(x, w)


def _cap(t):
    return GATE_CAP * jnp.tanh(t / GATE_CAP)


def _log_sigmoid(x):
    return jnp.minimum(x, 0.0) - jnp.log1p(jnp.exp(-jnp.abs(x)))


def _mlstm_kernel(q_ref, k_ref, v_ref, o_ref, g_ref, bi_ref, bf_ref, mhw_ref, c0_ref, n0_ref, m0_ref,
                  y_ref, c_ref, n_ref, m_ref, *, L, H, DK, DV):
    @pl.when(pl.program_id(1) == 0)
    def _():
        c_ref[...] = c0_ref[...]
        n_ref[...] = n0_ref[...]
        m_ref[...] = m0_ref[...]

    gates = g_ref[...]
    li_all = _cap(gates + bi_ref[...])
    lf_all = _log_sigmoid(_cap(gates + bf_ref[...]))
    causal = _iota((L, L), 1) <= _iota((L, L), 0)
    b_all = _dot_hi(causal.astype(f32), lf_all)
    sel = (_iota((8, LANES), 0) == _iota((8, LANES), 1)).astype(f32)
    li_rows = _dot_hi(sel, li_all, NT)
    b_rows = _dot_hi(sel, b_all, NT)

    for h in range(H):
        q = q_ref[:, h * DK:(h + 1) * DK]
        k = k_ref[:, h * DK:(h + 1) * DK] * (DK ** -0.5)
        v = v_ref[:, h * DV:(h + 1) * DV]
        bcol = b_all[:, H + h:H + h + 1]
        licol = li_all[:, h:h + 1]
        brow = b_rows[H + h:H + h + 1, :]
        lirow = li_rows[h:h + 1, :]
        m_prev = m_ref[0, :, h:h + 1]
        log_d = jnp.where(causal, bcol - brow + lirow, -jnp.inf)
        m_inter = m_prev + bcol
        m_t = jnp.maximum(m_inter, jnp.max(log_d, axis=-1, keepdims=True))
        s = _dot(q, k, NT) * jnp.exp(log_d - m_t)
        scale = jnp.exp(m_inter - m_t)
        c_prev = c_ref[0, h]
        n_prev = n_ref[0, h:h + 1, :]
        num = _dot(s, v) + scale * _dot(q, c_prev)
        den = jnp.sum(s, axis=-1, keepdims=True) + scale * jnp.sum(q * n_prev, axis=-1, keepdims=True)
        hh = num / jnp.maximum(jnp.abs(den), jnp.exp(-m_t))
        b_end = bcol[L - 1:L, :]
        g_end = b_end - bcol + licol
        m_new = jnp.maximum(m_prev + b_end, jnp.max(g_end, axis=0, keepdims=True))
        wts = jnp.exp(g_end - m_new)
        dec = jnp.exp(m_prev + b_end - m_new)
        kw = k * wts
        c_ref[0, h] = dec * c_prev + _dot(kw, v, TN)
        n_ref[0, h:h + 1, :] = dec * n_prev + jnp.sum(kw, axis=0, keepdims=True)
        m_ref[0, :, h:h + 1] = m_new
        hn = hh * lax.rsqrt(jnp.mean(hh * hh, axis=-1, keepdims=True) + MH_EPS)
        gate_o = jax.nn.sigmoid(o_ref[:, h * DV:(h + 1) * DV])
        y_ref[:, h * DV:(h + 1) * DV] = (hn * mhw_ref[:, h * DV:(h + 1) * DV] * gate_o).astype(y_ref.dtype)


def mlstm(zb, row0, B, T, L, bias_i, bias_f, mh_w, c0, n0, m0):
    _, H, DK, DV = c0.shape
    assert 2 * H <= 8 and T % L == 0 and row0 % L == 0 and DV == 2 * DK
    nc = T // L
    r0 = row0 // L
    rows = lambda b, c: r0 + b * nc + c
    kern = functools.partial(_mlstm_kernel, L=L, H=H, DK=DK, DV=DV)
    qk_w, v_w = H * DK, H * DV
    gate_blk = (2 * qk_w + 2 * v_w) // LANES
    return pl.pallas_call(
        kern,
        grid=(B, nc),
        in_specs=[
            pl.BlockSpec((L, qk_w), lambda b, c: (rows(b, c), 0)),
            pl.BlockSpec((L, qk_w), lambda b, c: (rows(b, c), 1)),
            pl.BlockSpec((L, v_w), lambda b, c: (rows(b, c), 1)),
            pl.BlockSpec((L, v_w), lambda b, c: (rows(b, c), 2)),
            pl.BlockSpec((L, LANES), lambda b, c: (rows(b, c), gate_blk)),
            pl.BlockSpec((1, LANES), lambda b, c: (0, 0)),
            pl.BlockSpec((1, LANES), lambda b, c: (0, 0)),
            pl.BlockSpec((1, v_w), lambda b, c: (0, 0)),
            pl.BlockSpec((1, H, DK, DV), lambda b, c: (b, 0, 0, 0)),
            pl.BlockSpec((1, H, DK), lambda b, c: (b, 0, 0)),
            pl.BlockSpec((1, 1, H), lambda b, c: (b, 0, 0)),
        ],
        out_specs=[
            pl.BlockSpec((L, v_w), lambda b, c: (b * nc + c, 0)),
            pl.BlockSpec((1, H, DK, DV), lambda b, c: (b, 0, 0, 0)),
            pl.BlockSpec((1, H, DK), lambda b, c: (b, 0, 0)),
            pl.BlockSpec((1, 1, H), lambda b, c: (b, 0, 0)),
        ],
        out_shape=[
            jax.ShapeDtypeStruct((B * T, v_w), bf16),
            jax.ShapeDtypeStruct((B, H, DK, DV), f32),
            jax.ShapeDtypeStruct((B, H, DK), f32),
            jax.ShapeDtypeStruct((B, 1, H), f32),
        ],
        compiler_params=_params("parallel", "arbitrary"),
        name="mlstm",
    )(zb, zb, zb, zb, zb, bias_i, bias_f, mh_w, c0, n0, m0.reshape(B, 1, H))


def _softplus(x):
    return jnp.maximum(x, 0.0) + jnp.log1p(jnp.exp(-jnp.abs(x)))


def _dot_sel(x, sel):
    xh, xl = _split(x)
    s = sel.astype(bf16)
    return jnp.dot(xh, s, preferred_element_type=f32) + jnp.dot(xl, s, preferred_element_type=f32)


def _head_blocks(gw, scale):
    return jnp.where(_iota((gw, gw), 0) // HEAD_A == _iota((gw, gw), 1) // HEAD_A, scale, 0.0).astype(f32)


def _d1(a, b, dims=NN):
    return lax.dot_general(a, b, dims, preferred_element_type=f32)


def _d3(a, b, dims=NN):
    return _d1(a[0], b[0], dims) + _d1(a[0], b[1], dims) + _d1(a[1], b[0], dims)


def _rows(x, sl):
    return tuple(t[sl] for t in x)


def _cat(xs):
    return tuple(jnp.concatenate(ts, axis=0) for ts in zip(*xs))


def _rwkv_kernel(r_ref, k_ref, v_ref, l_ref, sp_ref, mu_ref, w0_ref, a0_ref, kk_ref, ka_ref, rk_ref,
                 lw_ref, lb_ref, wup_ref, aup_ref, gup_ref, s0_ref,
                 o_ref, s_ref, sh_ref,
                 last_r, last_k, last_v, last_l, at_sc, bt_sc, kt_sc, rt_sc, bh_sc, kh_sc, v_sc, gam_sc,
                 y_sc, g_sc, bonus_sc, h_sc, *, L, S, U, DA, WP, AP, TLW):
    NP = DA // LANES
    GW = min(2 * LANES, DA)
    G = HEAD_A // L
    lane = _iota((1, LANES), 1)
    m0 = (lane < HEAD_A).astype(f32)
    m1 = 1.0 - m0

    @pl.when(pl.program_id(1) == 0)
    def _():
        last_r[...] = sp_ref[:, 0, 0:DA]
        last_k[...] = sp_ref[:, 0, DA:2 * DA]
        last_v[...] = sp_ref[:, 0, 2 * DA:3 * DA]
        last_l[...] = sp_ref[:, 0, 3 * DA:3 * DA + TLW]

        def load_state(i, carry):
            s, p = i // NP, i % NP
            hc = s0_ref[s, p]
            h_sc[s, p] = jnp.concatenate([hc * m0, hc * m1], axis=0)
            return carry

        lax.fori_loop(0, S * NP, load_state, 0)

    def shifted(ref, last, mu):
        pieces = []
        for s in range(S):
            cur = ref[s * L:(s + 1) * L, :]
            prev = jnp.where(_iota(cur.shape, 0) == 0, last[s:s + 1, :], pltpu.roll(cur, 1, 0))
            last[s:s + 1, :] = cur[L - 1:L, :]
            pieces.append(cur + mu * (prev - cur))
        return jnp.concatenate(pieces, axis=0)

    r = shifted(r_ref, last_r, mu_ref[:, 0:DA])
    k = shifted(k_ref, last_k, mu_ref[:, DA:2 * DA])
    v = shifted(v_ref, last_v, mu_ref[:, 2 * DA:3 * DA])
    xl = shifted(l_ref, last_l, mu_ref[:, 3 * DA:3 * DA + TLW])
    xw, xa, xg = xl[:, 0:WP], xl[:, WP:WP + AP], xl[:, WP + AP:]

    w_log = -_softplus(-(w0_ref[...] + _dot3(jnp.tanh(xw), wup_ref[...]))) - 0.5
    logw = -jnp.exp(w_log)
    a = jax.nn.sigmoid(a0_ref[...] + _dot3(xa, aup_ref[...]))
    g_sc[...] = _dot(jax.nn.sigmoid(xg), gup_ref[...])

    ones_bd = _head_blocks(GW, 1.0)
    seg_sum = lambda x: jnp.concatenate(
        [_dot_sel(x[:, i * GW:(i + 1) * GW], ones_bd) for i in range(DA // GW)], axis=1)
    kk = k * kk_ref[...]
    kk = kk / jnp.maximum(jnp.sqrt(seg_sum(kk * kk)), 1e-12)
    k_mod = k * (1.0 + (a - 1.0) * ka_ref[...])
    bonus_sc[...] = seg_sum(r * k_mod * rk_ref[...]) * v

    R = S * L
    rr, rc = _iota((R, R), 0), _iota((R, R), 1)
    tri = ((rr // L == rc // L) & (rc <= rr)).astype(f32)
    cs = _dot_hi(tri, logw)
    cs_last = [cs[s * L + L - 1:s * L + L, :] for s in range(S)]
    cs_end = jnp.concatenate([jnp.broadcast_to(t, (L, DA)) for t in cs_last], axis=0)
    e_neg = jnp.exp(-cs)
    e_end = jnp.exp(cs_end - cs)
    bv = kk * a
    vals = (
        (at_sc, -kk * jnp.exp(cs - logw)), (bt_sc, bv * e_neg), (kt_sc, k_mod * e_neg), (rt_sc, r * jnp.exp(cs)),
        (bh_sc, bv * e_end), (kh_sc, k_mod * e_end), (v_sc, v),
    )
    gam = jnp.exp(jnp.concatenate(cs_last, axis=0))
    for p in range(NP):
        sl = slice(p * LANES, (p + 1) * LANES)
        for ref, val in vals:
            ref[p] = val[:, sl]
        gam_sc[p] = gam[:, sl]

    SR = 2 * G * L
    ri = _iota((SR, SR), 0)
    ci = _iota((SR, SR), 1)
    same_blk = ri // L == ci // L
    strict = same_blk & (ci < ri)
    incl = same_blk & (ci <= ri)
    eye = (ri == ci).astype(f32)
    eye_l = _iota((LANES, LANES), 0) == _iota((LANES, LANES), 1)
    n_double = int(math.log2(L)) - 1

    seq_rows = [slice(j * 2 * L, (j + 1) * 2 * L) for j in range(G)]

    def pair_body(i, carry):
        pairs = [i * U + q for q in range(U)]
        chains = [(p, s0) for p in pairs for s0 in range(0, S, G)]
        each = lambda f, *lists: [f(*t) for t in zip(*lists)]
        hsp = [[_split(h_sc[s0 + j, p]) for j in range(G)] for p, s0 in chains]

        def stack(ref):
            out = []
            for p, s0 in chains:
                x = ref[p]
                parts = []
                for s in range(s0, s0 + G):
                    xs = x[s * L:(s + 1) * L]
                    parts += [xs * m0, xs * m1]
                out.append(_split(jnp.concatenate(parts, axis=0)))
            return out

        hi = lambda xs: [x[0] for x in xs]
        cast = lambda xs: [x.astype(bf16) for x in xs]
        la, lr, bt, kt = hi(stack(at_sc)), hi(stack(rt_sc)), hi(stack(bt_sc)), hi(stack(kt_sc))
        vs = stack(v_sc)
        n_ab = each(lambda a, b: jnp.where(strict, _d1(a, b, NT), 0.0), la, bt)
        a_ak = each(lambda a, k: jnp.where(strict, _d1(a, k, NT), 0.0), la, kt)
        xa = each(lambda a, h: jnp.concatenate([_d1(a[sl], h[j][0]) for j, sl in enumerate(seq_rows)], axis=0),
                  la, hsp)
        w = each(lambda x, a, v: x + _d1(a, v[0]), xa, cast(a_ak), vs)
        t_inv = [eye + n for n in n_ab]
        n_pow = cast(n_ab)
        for _ in range(n_double):
            n_pow = cast(each(lambda n: _d1(n, n), n_pow))
            t_inv = each(lambda t, n, tb: t + _d1(n, tb), t_inv, n_pow, cast(t_inv))
        u = each(lambda t, ww: _split(_d1(t, ww)), cast(t_inv), cast(w))
        r_b = cast(each(lambda r, b: jnp.where(incl, _d1(r, b, NT), 0.0), lr, bt))
        r_k = cast(each(lambda r, k: jnp.where(incl, _d1(r, k, NT), 0.0), lr, kt))
        xr = each(lambda r, h: jnp.concatenate([_d1(r[sl], h[j][0]) for j, sl in enumerate(seq_rows)], axis=0),
                  lr, hsp)
        y_st = each(lambda x, rb, uu, rk, v: x + _d1(rb, uu[0]) + _d1(rk, v[0]), xr, r_b, u, r_k, vs)
        lbh, lkh = stack(bh_sc), stack(kh_sc)
        h_new = []
        for ci, (p, s0) in enumerate(chains):
            for j, sl in enumerate(seq_rows):
                dg = _split(jnp.where(eye_l, gam_sc[p][s0 + j:s0 + j + 1, :], 0.0))
                lhs = _cat([_rows(lbh[ci], sl), _rows(lkh[ci], sl), dg])
                rhs = _cat([_rows(u[ci], sl), _rows(vs[ci], sl), hsp[ci][j]])
                h_new.append((s0 + j, p, _d3(lhs, rhs, TN)))
        per_pair = S // G
        for q, p in enumerate(pairs):
            ys = []
            for ci in range(q * per_pair, (q + 1) * per_pair):
                ys += [y_st[ci][j * 2 * L:j * 2 * L + L] + y_st[ci][j * 2 * L + L:(j + 1) * 2 * L] for j in range(G)]
            y_sc[p] = jnp.concatenate(ys, axis=0)
        for s, p, h in h_new:
            h_sc[s, p] = h
        return carry

    lax.fori_loop(0, NP // U, pair_body, 0)

    avg_bd = _head_blocks(GW, 1.0 / HEAD_A)
    for i in range(DA // GW):
        sl = slice(i * GW, (i + 1) * GW)
        y = jnp.concatenate([y_sc[i * (GW // LANES) + j] for j in range(GW // LANES)], axis=1)
        d = y - _dot_sel(y, avg_bd)
        yn = d * lax.rsqrt(_dot_sel(d * d, avg_bd) + GN_EPS)
        out = (yn * lw_ref[:, sl] + lb_ref[:, sl] + bonus_sc[:, sl]) * g_sc[:, sl]
        o_ref[:, sl] = out.astype(o_ref.dtype)

    @pl.when(pl.program_id(1) == pl.num_programs(1) - 1)
    def _():
        def store_state(i, carry):
            s, p = i // NP, i % NP
            hbd = h_sc[s, p]
            s_ref[s, p] = hbd[:HEAD_A] + hbd[HEAD_A:]
            return carry

        lax.fori_loop(0, S * NP, store_state, 0)
        sh_ref[:, 0, 0:DA] = last_r[...]
        sh_ref[:, 0, DA:2 * DA] = last_k[...]
        sh_ref[:, 0, 2 * DA:3 * DA] = last_v[...]
        sh_ref[:, 0, 3 * DA:3 * DA + TLW] = last_l[...]


def rwkv(za, row0, B, T, L, S, shift_prev, s0, mu, w0, a0, k_k, k_a, r_k, lnx_w, lnx_b, wup, aup, gup, U=2):
    DA = w0.shape[-1]
    WP, AP = wup.shape[0], aup.shape[0]
    TLW = WP + AP + gup.shape[0]
    NA = 3 * DA + TLW
    R = S * L
    G = HEAD_A // L
    NP = DA // LANES
    assert za.shape[1] == NA and (3 * DA) % TLW == 0 and L & (L - 1) == 0 and 8 <= L <= HEAD_A
    assert T % L == 0 and B % S == 0 and S % G == 0 and NP % U == 0 and (S == 1 or T == L) and row0 % R == 0
    nc = T // L
    r0 = row0 // R
    rows = lambda b, c: r0 + b * nc + c
    kern = functools.partial(_rwkv_kernel, L=L, S=S, U=U, DA=DA, WP=WP, AP=AP, TLW=TLW)
    vec = pl.BlockSpec((1, DA), lambda b, c: (0, 0))
    full = lambda arr: pl.BlockSpec(arr.shape, lambda b, c: (0,) * arr.ndim)
    state = pl.BlockSpec((S, NP, HEAD_A, LANES), lambda b, c: (b, 0, 0, 0))
    pair_sc = pltpu.VMEM((NP, R, LANES), f32)
    return pl.pallas_call(
        kern,
        grid=(B // S, nc),
        in_specs=[
            pl.BlockSpec((R, DA), lambda b, c: (rows(b, c), 0)),
            pl.BlockSpec((R, DA), lambda b, c: (rows(b, c), 1)),
            pl.BlockSpec((R, DA), lambda b, c: (rows(b, c), 2)),
            pl.BlockSpec((R, TLW), lambda b, c: (rows(b, c), 3 * DA // TLW)),
            pl.BlockSpec((S, 1, NA), lambda b, c: (b, 0, 0)),
            pl.BlockSpec((1, NA), lambda b, c: (0, 0)),
            vec, vec, vec, vec, vec, vec, vec,
            full(wup), full(aup), full(gup),
            state,
        ],
        out_specs=[pl.BlockSpec((R, DA), lambda b, c: (b * nc + c, 0)), state,
                   pl.BlockSpec((S, 1, NA), lambda b, c: (b, 0, 0))],
        out_shape=[jax.ShapeDtypeStruct((B * T, DA), bf16), jax.ShapeDtypeStruct(s0.shape, f32),
                   jax.ShapeDtypeStruct((B, 1, NA), f32)],
        scratch_shapes=[
            pltpu.VMEM((S, DA), f32), pltpu.VMEM((S, DA), f32), pltpu.VMEM((S, DA), f32), pltpu.VMEM((S, TLW), f32),
            pair_sc, pair_sc, pair_sc, pair_sc, pair_sc, pair_sc, pair_sc, pltpu.VMEM((NP, S, LANES), f32),
            pair_sc, pltpu.VMEM((R, DA), f32), pltpu.VMEM((R, DA), f32),
            pltpu.VMEM((S, NP, LANES, LANES), f32),
        ],
        compiler_params=_params("arbitrary", "arbitrary"),
        name="rwkv",
    )(za, za, za, za, shift_prev, mu, w0, a0, k_k, k_a, r_k, lnx_w, lnx_b, wup, aup, gup, s0)


def _merge_kernel(ya1_ref, ya2_ref, yb1_ref, yb2_ref, pa_ref, pb_ref, ga_ref, gb_ref, ba_ref, bb_ref, o_ref, *,
                  NA_BLOCKS):
    ga = jax.nn.sigmoid(ga_ref[...] + ba_ref[...])
    gb = jax.nn.sigmoid(gb_ref[...] + bb_ref[...])

    def body(ya_ref, yb_ref):
        pa = jnp.dot(ya_ref[...], pa_ref[...], preferred_element_type=f32)
        pb = jnp.dot(yb_ref[...], pb_ref[...], preferred_element_type=f32)
        o_ref[...] = (ga * pa + gb * pb).astype(o_ref.dtype)

    pl.when(pl.program_id(1) < NA_BLOCKS)(lambda: body(ya1_ref, yb1_ref))
    pl.when(pl.program_id(1) >= NA_BLOCKS)(lambda: body(ya2_ref, yb2_ref))


def merge(ya, yb, p_a, p_b, zg, b_gate, tm=512, tn=512):
    n = ya[0].shape[0] + ya[1].shape[0]
    da, db = ya[0].shape[1], yb[0].shape[1]
    d = p_a.shape[1]
    nj = d // tn
    assert ya[0].shape[0] % tm == 0 and ya[1].shape[0] % tm == 0
    na = ya[0].shape[0] // tm
    first = lambda j, i: (jnp.minimum(i, na - 1), 0)
    second = lambda j, i: (jnp.maximum(i - na, 0), 0)
    return pl.pallas_call(
        functools.partial(_merge_kernel, NA_BLOCKS=na),
        grid=(nj, n // tm),
        in_specs=[
            pl.BlockSpec((tm, da), first),
            pl.BlockSpec((tm, da), second),
            pl.BlockSpec((tm, db), first),
            pl.BlockSpec((tm, db), second),
            pl.BlockSpec((da, tn), lambda j, i: (0, j)),
            pl.BlockSpec((db, tn), lambda j, i: (0, j)),
            pl.BlockSpec((tm, tn), lambda j, i: (i, j)),
            pl.BlockSpec((tm, tn), lambda j, i: (i, nj + j)),
            pl.BlockSpec((1, tn), lambda j, i: (0, j)),
            pl.BlockSpec((1, tn), lambda j, i: (0, nj + j)),
        ],
        out_specs=pl.BlockSpec((tm, tn), lambda j, i: (i, j)),
        out_shape=jax.ShapeDtypeStruct((n, d), bf16),
        compiler_params=_params("arbitrary", "arbitrary"),
        name="merge",
    )(ya[0], ya[1], yb[0], yb[1], p_a, p_b, zg, zg, b_gate, b_gate)


def _outproj_kernel(u_ref, w_ref, xa_ref, xb_ref, o_ref, *, NA_BLOCKS):
    acc = jnp.dot(u_ref[...], w_ref[...], preferred_element_type=f32)

    @pl.when(pl.program_id(1) < NA_BLOCKS)
    def _():
        o_ref[...] = xa_ref[...] + acc

    @pl.when(pl.program_id(1) >= NA_BLOCKS)
    def _():
        o_ref[...] = xb_ref[...] + acc


def outproj(u, w, xa, xb, tm=512, tn=512):
    n, k = u.shape
    d = w.shape[1]
    assert xa.shape[0] % tm == 0 and xa.shape[0] + xb.shape[0] == n
    na = xa.shape[0] // tm
    return pl.pallas_call(
        functools.partial(_outproj_kernel, NA_BLOCKS=na),
        grid=(d // tn, n // tm),
        in_specs=[pl.BlockSpec((tm, k), lambda j, i: (i, 0)), pl.BlockSpec((k, tn), lambda j, i: (0, j))]
        + _two_source_specs(tm, tn, na, 2),
        out_specs=pl.BlockSpec((tm, tn), lambda j, i: (i, j)),
        out_shape=jax.ShapeDtypeStruct((n, d), f32),
        compiler_params=_params("arbitrary", "arbitrary"),
        name="outproj",
    )(u, w, xa, xb)


def _router_kernel(x_ref, g_ref, wr_ref, br_ref, h_ref, idx_ref, gate_ref, *, E):
    x = x_ref[...]
    h = x * lax.rsqrt(jnp.mean(x * x, axis=-1, keepdims=True) + NORM_EPS) * g_ref[...]
    h_ref[...] = _pack_bf16_halves(h)
    logits = _dot_hi(h, wr_ref[...]) + br_ref[...]
    lane = _iota(logits.shape, 1)
    l = jnp.where(lane < E, logits, -jnp.inf)
    vals, idxs = [], []
    for _ in range(TOP_K):
        mx = jnp.max(l, axis=-1, keepdims=True)
        ix = jnp.min(jnp.where(l == mx, lane, LANES), axis=-1, keepdims=True)
        vals.append(mx)
        idxs.append(ix)
        l = jnp.where(lane == ix, -jnp.inf, l)
    es = [jnp.exp(v - vals[0]) for v in vals]
    tot = functools.reduce(lambda a, b: a + b, es)
    gate_out = jnp.zeros(logits.shape, f32)
    idx_out = jnp.zeros(logits.shape, i32)
    for k in range(TOP_K):
        gate_out = jnp.where(lane == k, es[k] / tot, gate_out)
        idx_out = jnp.where(lane == k, idxs[k], idx_out)
    gate_ref[...] = gate_out
    idx_ref[...] = idx_out


def router(x, g, w_router, b_router, tm=256):
    n, d = x.shape
    E = w_router.shape[1]
    wr = jnp.pad(w_router, ((0, 0), (0, LANES - E)))
    br = jnp.pad(b_router, (0, LANES - E)).reshape(1, LANES)
    return pl.pallas_call(
        functools.partial(_router_kernel, E=E),
        grid=(n // tm,),
        in_specs=[
            pl.BlockSpec((tm, d), lambda i: (i, 0)),
            pl.BlockSpec((1, d), lambda i: (0, 0)),
            pl.BlockSpec((d, LANES), lambda i: (0, 0)),
            pl.BlockSpec((1, LANES), lambda i: (0, 0)),
        ],
        out_specs=[
            pl.BlockSpec((tm, d // 2), lambda i: (i, 0)),
            pl.BlockSpec((tm, LANES), lambda i: (i, 0)),
            pl.BlockSpec((tm, LANES), lambda i: (i, 0)),
        ],
        out_shape=[
            jax.ShapeDtypeStruct((n, d // 2), jnp.uint32),
            jax.ShapeDtypeStruct((n, LANES), i32),
            jax.ShapeDtypeStruct((n, LANES), f32),
        ],
        compiler_params=_params("parallel"),
        name="router",
    )(x, g.reshape(1, d), wr, br)


def _route(idx, E, tb):
    n, k = idx.shape
    sel = idx[:, :, None] == jnp.arange(E, dtype=i32)[None, None, :]
    onehot = jnp.sum(sel.astype(i32), axis=1)
    pos = jnp.cumsum(onehot, axis=0) - onehot
    counts = jnp.sum(onehot, axis=0)
    padded = (counts + tb - 1) // tb * tb
    pend = jnp.cumsum(padded)
    pstart = pend - padded
    dest = jnp.sum(jnp.where(sel, (pstart[None, :] + pos)[:, None, :], 0), axis=2).astype(i32)
    nb = n * k // tb + E
    tok = (jnp.arange(nb * tb, dtype=i32) % n).at[dest.reshape(-1)].set(jnp.repeat(jnp.arange(n, dtype=i32), k))
    first_row = jnp.arange(nb, dtype=i32) * tb
    block_e = jnp.minimum(jnp.sum((pend[None, :] <= first_row[:, None]).astype(i32), axis=1), E - 1)
    n_used = (pend[-1] // tb).astype(i32)
    group_end = pend[block_e] // tb
    next_e = jnp.where(group_end < n_used, block_e[jnp.minimum(group_end, nb - 1)], -1).astype(i32)
    rows_valid = jnp.clip((pstart + counts)[block_e] - first_row, 0, tb)
    rows_valid = jnp.where(first_row < pend[-1], rows_valid, 0).astype(i32)
    return dest, tok, block_e, rows_valid, next_e


def _gather_kernel(tok_ref, nv_ref, src_hbm, o_ref, buf, sem, *, RB):
    i = pl.program_id(0)
    nblk = pl.num_programs(0)

    HB = RB // 2

    def rows_to_fetch(blk):
        return jnp.where(nv_ref[blk] > HB, RB, HB)

    def issue_block(blk):
        slot = blk % 2

        def issue(r, carry):
            row = tok_ref[blk * RB + r]
            pltpu.make_async_copy(src_hbm.at[pl.ds(row, 1)], buf.at[slot, pl.ds(r, 1)], sem.at[slot]).start()
            return carry

        lax.fori_loop(0, rows_to_fetch(blk), issue, 0)

    @pl.when(i == 0)
    def _():
        issue_block(0)

    @pl.when((i + 1 < nblk) & (nv_ref[jnp.minimum(i + 1, nblk - 1)] > 0))
    def _():
        issue_block(i + 1)

    slot = i % 2

    def unpack(rows):
        hi, lo = _unpack_bf16_halves(buf[slot, rows, :])
        half = hi.shape[1]
        o_ref[rows, :half] = hi.astype(o_ref.dtype)
        o_ref[rows, half:] = lo.astype(o_ref.dtype)

    @pl.when(nv_ref[i] > HB)
    def _():
        pltpu.make_async_copy(src_hbm.at[pl.ds(0, RB)], buf.at[slot], sem.at[slot]).wait()
        unpack(slice(0, RB))

    @pl.when((nv_ref[i] > 0) & (nv_ref[i] <= HB))
    def _():
        pltpu.make_async_copy(src_hbm.at[pl.ds(0, HB)], buf.at[slot, pl.ds(0, HB)], sem.at[slot]).wait()
        unpack(slice(0, HB))
        o_ref[HB:, :] = jnp.zeros((RB - HB, o_ref.shape[1]), o_ref.dtype)

    @pl.when(nv_ref[i] == 0)
    def _():
        o_ref[...] = jnp.zeros(o_ref.shape, o_ref.dtype)


def gather_rows(src, tok, rows_valid, rb=256):
    p = tok.shape[0]
    d = 2 * src.shape[1]
    return pl.pallas_call(
        functools.partial(_gather_kernel, RB=rb),
        grid_spec=pltpu.PrefetchScalarGridSpec(
            num_scalar_prefetch=2,
            grid=(p // rb,),
            in_specs=[pl.BlockSpec(memory_space=pl.ANY)],
            out_specs=pl.BlockSpec((rb, d), lambda i, tok, nv: (i, 0)),
            scratch_shapes=[pltpu.VMEM((2, rb, d // 2), src.dtype), pltpu.SemaphoreType.DMA((2,))],
        ),
        out_shape=jax.ShapeDtypeStruct((p, d), bf16),
        compiler_params=_params("arbitrary"),
        name="gather_rows",
    )(tok, rows_valid, src)


def _expert_block(be_ref, nu_ref, nx_ref, tile_copy, n_tiles, o_ref, compute):
    j, i = pl.program_id(0), pl.program_id(1)
    nj = pl.num_programs(0)
    n_valid = nu_ref[i]
    tb = o_ref.shape[0]

    def fetch(e, jj):
        for t in range(n_tiles):
            tile_copy(e, jj, t).start()

    @pl.when((j == 0) & (i == 0))
    def _():
        fetch(be_ref[0], 0)

    first = (n_valid > 0) & ((i == 0) | (be_ref[i] != be_ref[jnp.maximum(i - 1, 0)]))

    @pl.when(first)
    def _():
        for t in range(n_tiles):
            tile_copy(0, 0, t).wait()
        _on_valid_rows(n_valid, tb, o_ref, functools.partial(compute, fresh=True))
        nxt = nx_ref[i]

        @pl.when(nxt >= 0)
        def _():
            fetch(nxt, j)

        @pl.when((nxt < 0) & (j + 1 < nj))
        def _():
            fetch(be_ref[0], j + 1)

    @pl.when(jnp.logical_not(first))
    def _():
        _on_valid_rows(n_valid, tb, o_ref, functools.partial(compute, fresh=False))


def _on_valid_rows(n_valid, tb, o_ref, compute):
    half = tb // 2

    def zero(rows):
        o_ref[rows, :] = jnp.zeros((rows.stop - rows.start, o_ref.shape[1]), o_ref.dtype)

    @pl.when(n_valid > half)
    def _():
        compute(slice(0, tb))

    @pl.when((n_valid > 0) & (n_valid <= half))
    def _():
        compute(slice(0, half))
        zero(slice(half, tb))

    @pl.when(n_valid == 0)
    def _():
        zero(slice(0, tb))


def _chunked_dots(x_ref, rows, land, w_bf, n_tiles, fresh):
    if not fresh:
        x = x_ref[rows, :]
        return [jnp.dot(x, w_bf[t], preferred_element_type=f32) for t in range(n_tiles)]
    k_total = x_ref.shape[1]
    chunk = math.gcd(k_total, CAST_CHUNK)
    acc = [None] * n_tiles
    for k0 in range(0, k_total, chunk):
        ks = slice(k0, k0 + chunk)
        xk = x_ref[rows, ks]
        for t in range(n_tiles):
            wk = land[t, ks, :].astype(bf16)
            w_bf[t, ks, :] = wk
            part = jnp.dot(xk, wk, preferred_element_type=f32)
            acc[t] = part if acc[t] is None else acc[t] + part
    return acc


def _expert_up_kernel(be_ref, nu_ref, nx_ref, x_ref, w_hbm, bg_ref, bl_ref, o_ref, land, w_bf, sem, *, tf, f):
    def tile_copy(e, jj, t):
        col = pl.multiple_of(t * f + jj * tf, LANES)
        return pltpu.make_async_copy(w_hbm.at[e, :, pl.ds(col, tf)], land.at[t], sem.at[t])

    def compute(rows, fresh):
        gate, lin = _chunked_dots(x_ref, rows, land, w_bf, 2, fresh)
        gate = jnp.minimum(gate + bg_ref[...], SWIGLU_LIMIT)
        lin = jnp.clip(lin + bl_ref[...], -SWIGLU_LIMIT, SWIGLU_LIMIT)
        o_ref[rows, :] = (gate * jax.nn.sigmoid(SWIGLU_ALPHA * gate) * (lin + 1.0)).astype(o_ref.dtype)

    _expert_block(be_ref, nu_ref, nx_ref, tile_copy, 2, o_ref, compute)


def expert_up(xg, w1, b1, block_e, rows_valid, next_e, tb, tf=512):
    p, d = xg.shape
    E, _, f2 = w1.shape
    f = f2 // 2
    nb = p // tb
    nj = f // tf
    return pl.pallas_call(
        functools.partial(_expert_up_kernel, tf=tf, f=f),
        grid_spec=pltpu.PrefetchScalarGridSpec(
            num_scalar_prefetch=3,
            grid=(nj, nb),
            in_specs=[
                pl.BlockSpec((tb, d), lambda j, i, be, nu, nx: (i, 0)),
                pl.BlockSpec(memory_space=pl.ANY),
                pl.BlockSpec((None, 1, tf), lambda j, i, be, nu, nx: (be[i], 0, j)),
                pl.BlockSpec((None, 1, tf), lambda j, i, be, nu, nx: (be[i], 0, nj + j)),
            ],
            out_specs=pl.BlockSpec((tb, tf), lambda j, i, be, nu, nx: (i, j)),
            scratch_shapes=[pltpu.VMEM((2, d, tf), f32), pltpu.VMEM((2, d, tf), bf16), pltpu.SemaphoreType.DMA((2,))],
        ),
        out_shape=jax.ShapeDtypeStruct((p, f), bf16),
        compiler_params=_params("arbitrary", "arbitrary"),
        name="expert_up",
    )(block_e, rows_valid, next_e, xg, w1, b1.reshape(E, 1, f2), b1.reshape(E, 1, f2))


def _expert_down_kernel(be_ref, nu_ref, nx_ref, h_ref, w_hbm, b_ref, o_ref, land, w_bf, sem, *, td):
    def tile_copy(e, jj, t):
        col = pl.multiple_of(jj * td, LANES)
        return pltpu.make_async_copy(w_hbm.at[e, :, pl.ds(col, td)], land.at[t], sem.at[t])

    def compute(rows, fresh):
        (y,) = _chunked_dots(h_ref, rows, land, w_bf, 1, fresh)
        o_ref[rows, :] = _pack_bf16_halves(y + b_ref[...])

    _expert_block(be_ref, nu_ref, nx_ref, tile_copy, 1, o_ref, compute)


def expert_down(hid, w2, b2, block_e, rows_valid, next_e, tb, td=1024):
    p, f = hid.shape
    E, _, d = w2.shape
    return pl.pallas_call(
        functools.partial(_expert_down_kernel, td=td),
        grid_spec=pltpu.PrefetchScalarGridSpec(
            num_scalar_prefetch=3,
            grid=(d // td, p // tb),
            in_specs=[
                pl.BlockSpec((tb, f), lambda j, i, be, nu, nx: (i, 0)),
                pl.BlockSpec(memory_space=pl.ANY),
                pl.BlockSpec((None, 1, td), lambda j, i, be, nu, nx: (be[i], 0, j)),
            ],
            out_specs=pl.BlockSpec((tb, td // 2), lambda j, i, be, nu, nx: (i, j)),
            scratch_shapes=[pltpu.VMEM((1, f, td), f32), pltpu.VMEM((1, f, td), bf16), pltpu.SemaphoreType.DMA((1,))],
        ),
        out_shape=jax.ShapeDtypeStruct((p, d // 2), jnp.uint32),
        compiler_params=_params("arbitrary", "arbitrary"),
        name="expert_down",
    )(block_e, rows_valid, next_e, hid, w2, b2.reshape(E, 1, d))


def _combine_kernel(dest_ref, x_ref, gate_ref, nf_ref, y_hbm, oa_ref, ob_ref, buf, sem, *, TM, NA_BLOCKS, TD):
    i = pl.program_id(0)
    nblk = pl.num_programs(0)

    def issue_block(blk):
        slot = blk % 2

        def issue(t, carry):
            for k in range(TOP_K):
                row = dest_ref[(blk * TM + t) * TOP_K + k]
                pltpu.make_async_copy(y_hbm.at[pl.ds(row, 1)], buf.at[slot, k, pl.ds(t, 1)], sem.at[slot]).start()
            return carry

        lax.fori_loop(0, TM, issue, 0)

    @pl.when(i == 0)
    def _():
        issue_block(0)

    @pl.when(i + 1 < nblk)
    def _():
        issue_block(i + 1)

    slot = i % 2
    for k in range(TOP_K):
        pltpu.make_async_copy(y_hbm.at[pl.ds(0, TM)], buf.at[slot, k], sem.at[slot]).wait()
    acc_hi = acc_lo = None
    for k in range(TOP_K):
        hi, lo = _unpack_bf16_halves(buf[slot, k])
        g = gate_ref[:, k:k + 1]
        acc_hi = g * hi if acc_hi is None else acc_hi + g * hi
        acc_lo = g * lo if acc_lo is None else acc_lo + g * lo
    pieces = []
    for j in range(x_ref.shape[1] // TD):
        cols = slice(j * TD // 2, (j + 1) * TD // 2)
        pieces += [acc_hi[:, cols], acc_lo[:, cols]]
    acc = x_ref[...] + jnp.concatenate(pieces, axis=1)
    y = acc * lax.rsqrt(jnp.mean(acc * acc, axis=-1, keepdims=True) + NORM_EPS) * nf_ref[...]

    @pl.when(pl.program_id(0) < NA_BLOCKS)
    def _():
        oa_ref[...] = y

    @pl.when(pl.program_id(0) >= NA_BLOCKS)
    def _():
        ob_ref[...] = y


def combine(x1, gates, norm_final, yb, dest, n_first, td, tm=128):
    n, d = x1.shape
    na = n_first // tm
    assert n_first % tm == 0 and 0 < na < n // tm
    return pl.pallas_call(
        functools.partial(_combine_kernel, TM=tm, NA_BLOCKS=na, TD=td),
        grid_spec=pltpu.PrefetchScalarGridSpec(
            num_scalar_prefetch=1,
            grid=(n // tm,),
            in_specs=[
                pl.BlockSpec((tm, d), lambda i, dest: (i, 0)),
                pl.BlockSpec((tm, LANES), lambda i, dest: (i, 0)),
                pl.BlockSpec((1, d), lambda i, dest: (0, 0)),
                pl.BlockSpec(memory_space=pl.ANY),
            ],
            out_specs=[
                pl.BlockSpec((tm, d), lambda i, dest: (jnp.minimum(i, na - 1), 0)),
                pl.BlockSpec((tm, d), lambda i, dest: (jnp.maximum(i - na, 0), 0)),
            ],
            scratch_shapes=[pltpu.VMEM((2, TOP_K, tm, d // 2), jnp.uint32), pltpu.SemaphoreType.DMA((2,))],
        ),
        out_shape=[jax.ShapeDtypeStruct((n_first, d), f32), jax.ShapeDtypeStruct((n - n_first, d), f32)],
        compiler_params=_params("arbitrary"),
        name="combine",
    )(dest.reshape(-1), x1, gates, norm_final.reshape(1, d), yb)


def moe_and_final_norm(x1, norm_ffn, w_router, b_router, w1, b1, w2, b2, norm_final, n_first, tb, tf=512, td=512,
                       tm_router=256, tm_combine=128):
    h2, idx, gates = router(x1, norm_ffn, w_router, b_router, tm=tm_router)
    dest, tok, block_e, rows_valid, next_e = _route(idx[:, :TOP_K], w_router.shape[1], tb)
    xg = gather_rows(h2, tok, rows_valid, rb=tb)
    hid = expert_up(xg, w1, b1, block_e, rows_valid, next_e, tb, tf=tf)
    yb = expert_down(hid, w2, b2, block_e, rows_valid, next_e, tb, td=td)
    return combine(x1, gates, norm_final, yb, dest, n_first, td, tm=tm_combine)


def _pick_tile(m, cap=1024):
    units = m // LANES
    best = max(u for u in range(1, cap // LANES + 1) if units % u == 0)
    return best * LANES


def _pow2_chunk(t, cap):
    c = 1
    while c * 2 <= cap and t % (c * 2) == 0:
        c *= 2
    return c


def _state_to_pairs(s):
    B, H = s.shape[:2]
    t = jnp.swapaxes(s, 2, 3).reshape(B, H // 2, 2, HEAD_A, HEAD_A)
    return jnp.swapaxes(t, 2, 3).reshape(B, H // 2, HEAD_A, 2 * HEAD_A)


def _pairs_to_state(hc):
    B, NP = hc.shape[:2]
    t = jnp.swapaxes(hc.reshape(B, NP, HEAD_A, 2, HEAD_A), 2, 3)
    return jnp.swapaxes(t.reshape(B, 2 * NP, HEAD_A, HEAD_A), 2, 3)


def _pad_last(x, width):
    return jnp.pad(x, [(0, 0)] * (x.ndim - 1) + [(0, width - x.shape[-1])])


def kernel(x_prompt, x_sample, state_shift, state_rwkv, state_mlstm_c, state_mlstm_n, state_mlstm_m, norm_mix, w_in, mu_shift, w0, w_up, a0, a_up, g_up, k_k, k_a, r_k, lnx_w, lnx_b, b_igate, b_fgate, mh_norm, b_gate, p_a, p_b, w_out, norm_ffn, w_router, b_router, w_mlp1, b_mlp1, w_mlp2, b_mlp2, norm_final):
    assert norm_mix.shape[0] == 1, "single trunk layer"
    Bp, Tp, D = x_prompt.shape
    Bs, Ts, _ = x_sample.shape
    Np, Ns = Bp * Tp, Bs * Ts
    DA, LW, LA, LG = w0.shape[-1], w_up.shape[1], a_up.shape[1], g_up.shape[1]
    HA = r_k.shape[1]
    assert r_k.shape[2] == HEAD_A and HA * HEAD_A == DA and LG % LANES == 0
    _, _, HB, DK, DV = state_mlstm_c.shape
    DQK, DB = HB * DK, HB * DV
    n_shift = 3 * DA + LW + LA + LG
    n_ml = 2 * DQK + 2 * DB + 2 * HB
    WP, AP = _round_up(LW, LANES), _round_up(LA, LANES)

    def pad_shift_cols(t):
        o = 3 * DA
        return jnp.concatenate(
            [t[..., :o], _pad_last(t[..., o:o + LW], WP), _pad_last(t[..., o + LW:o + LW + LA], AP),
             t[..., o + LW + LA:]], axis=-1)

    def unpad_shift_cols(t):
        o = 3 * DA
        return jnp.concatenate([t[..., :o], t[..., o:o + LW], t[..., o + WP:o + WP + LA], t[..., o + WP + AP:]], axis=-1)

    w = w_in[0]
    wa = pad_shift_cols(w[:, :n_shift]).astype(bf16)
    ob = n_shift + 2 * DQK + 2 * DB
    wb = jnp.concatenate([w[:, n_shift:ob], _pad_last(w[:, ob:n_shift + n_ml], LANES)], axis=-1).astype(bf16)
    wg = w[:, n_shift + n_ml:].astype(bf16)

    xp, xs = x_prompt.reshape(Np, D), x_sample.reshape(Ns, D)
    tm = _pow2_chunk(math.gcd(Np, Ns), 512)
    h = rmsnorm(xp, xs, norm_mix[0], bf16, tm=tm)
    za = matmul(h, wa, _pick_tile(wa.shape[1]), tm=tm)
    zb = matmul(h, wb, _pick_tile(wb.shape[1]), tm=tm)
    zg = matmul(h, wg, _pick_tile(wg.shape[1]), tm=tm)

    row = lambda t: t.reshape(1, -1)
    rwkv_params = (row(pad_shift_cols(mu_shift[0])), row(w0[0]), row(a0[0]), row(k_k[0]), row(k_a[0]), row(r_k[0]),
                   row(lnx_w[0]), row(lnx_b[0]), jnp.pad(w_up[0], ((0, WP - LW), (0, 0))),
                   jnp.pad(a_up[0], ((0, AP - LA), (0, 0))), g_up[0])
    NA = wa.shape[1]

    def rwkv_group(row0, B, T, shift_prev, s0):
        L = _pow2_chunk(T, RWKV_CHUNK)
        S, U = (1, 4) if T > L else (HEAD_A // L, 2)
        return rwkv(za, row0, B, T, L, S, shift_prev, s0, *rwkv_params, U=U)

    ya_p, s_p, sh_p = rwkv_group(0, Bp, Tp, jnp.zeros((Bp, 1, NA), f32), jnp.zeros((Bp, HA // 2, HEAD_A, LANES), f32))
    ya_s, s_s, sh_s = rwkv_group(Np, Bs, Ts, pad_shift_cols(state_shift[0])[:, None, :],
                                 _state_to_pairs(state_rwkv[0]))
    s_p, s_s = _pairs_to_state(s_p), _pairs_to_state(s_s)

    bias_i = jnp.pad(b_igate[0], (0, LANES - HB)).reshape(1, LANES)
    bias_f = jnp.pad(b_fgate[0], (HB, LANES - 2 * HB)).reshape(1, LANES)
    mh_w = row(mh_norm[0])
    yb_p, c_p, n_p, m_p = mlstm(zb, 0, Bp, Tp, math.gcd(Tp, MLSTM_CHUNK), bias_i, bias_f, mh_w,
                                jnp.zeros((Bp, HB, DK, DV), f32), jnp.zeros((Bp, HB, DK), f32), jnp.zeros((Bp, HB), f32))
    yb_s, c_s, n_s, m_s = mlstm(zb, Np, Bs, Ts, math.gcd(Ts, MLSTM_CHUNK), bias_i, bias_f, mh_w,
                                state_mlstm_c[0], state_mlstm_n[0], state_mlstm_m[0])

    tn = _pick_tile(D, 512)
    u = merge((ya_p, ya_s), (yb_p, yb_s), p_a[0].astype(bf16), p_b[0].astype(bf16), zg, row(b_gate[0]), tm=tm, tn=tn)
    x1 = outproj(u, w_out[0].astype(bf16), xp, xs, tm=tm, tn=tn)
    y_p, y_s = moe_and_final_norm(x1, norm_ffn[0], w_router[0], b_router[0], w_mlp1[0], b_mlp1[0], w_mlp2[0],
                                  b_mlp2[0], norm_final, Np, tb=EXPERT_ROWS, tf=_pick_tile(w_mlp2.shape[2], 512),
                                  td=_pick_tile(D, 1024),
                                  tm_router=min(tm, 256), tm_combine=min(tm, 128))

    shift_p = unpad_shift_cols(sh_p[:, 0])
    shift_s = unpad_shift_cols(sh_s[:, 0])
    return (y_p.reshape(Bp, Tp, D), y_s.reshape(Bs, Ts, D),
            shift_p[None], s_p[None], c_p[None], n_p[None], m_p.reshape(1, Bp, HB),
            shift_s[None], s_s[None], c_s[None], n_s[None], m_s.reshape(1, Bs, HB))
```

```python
import functools
import math

import jax
import jax.numpy as jnp
from jax import lax
from jax.experimental import pallas as pl
from jax.experimental.pallas import tpu as pltpu

f32 = jnp.float32
bf16 = jnp.bfloat16
i32 = jnp.int32

LANES = 128
HEAD_A = 64
NORM_EPS = 1e-5
GN_EPS = 64e-5
MH_EPS = 1e-6
GATE_CAP = 15.0
TOP_K = 4
SWIGLU_LIMIT = 7.0
SWIGLU_ALPHA = 1.702
RWKV_CHUNK = 64
MLSTM_CHUNK = 128
EXPERT_ROWS = 512
CAST_CHUNK = 512
VMEM_LIMIT = 56 * 1024 * 1024

NN = (((1,), (0,)), ((), ()))
NT = (((1,), (1,)), ((), ()))
TN = (((0,), (0,)), ((), ()))


def _round_up(x, m):
    return (x + m - 1) // m * m


def _mx(x):
    if x.dtype == bf16 or x.shape[0] % 16 != 0:
        return x
    return x.astype(bf16)


def _dot(a, b, dims=NN):
    return lax.dot_general(_mx(a), _mx(b), dims, preferred_element_type=f32)


def _split(x):
    hi = x.astype(bf16)
    lo = (x - hi.astype(f32)).astype(bf16)
    return hi, lo


def _dot3(a, b, dims=NN):
    ah, al = _split(a)
    bh, bl = _split(b)
    d = lambda x, y: lax.dot_general(x, y, dims, preferred_element_type=f32)
    return d(ah, bh) + d(ah, bl) + d(al, bh)


def _dot_hi(a, b, dims=NN):
    return lax.dot_general(a, b, dims, preferred_element_type=f32, precision=lax.Precision.HIGHEST)


def _pack_bf16_halves(x):
    bits = lax.bitcast_convert_type(x.astype(bf16).astype(f32), jnp.uint32)
    half = x.shape[1] // 2
    return (bits[:, :half] & jnp.uint32(0xFFFF0000)) | (bits[:, half:] >> 16)


def _unpack_bf16_halves(packed):
    hi = lax.bitcast_convert_type(packed & jnp.uint32(0xFFFF0000), f32)
    lo = lax.bitcast_convert_type(packed << 16, f32)
    return hi, lo


def _iota(shape, dim):
    return lax.broadcasted_iota(i32, shape, dim)


def _params(*sem):
    return pltpu.CompilerParams(dimension_semantics=sem, vmem_limit_bytes=VMEM_LIMIT)


def _two_source_specs(tm, d, na, grid_rank):
    if grid_rank == 1:
        return [pl.BlockSpec((tm, d), lambda i: (jnp.minimum(i, na - 1), 0)),
                pl.BlockSpec((tm, d), lambda i: (jnp.maximum(i - na, 0), 0))]
    return [pl.BlockSpec((tm, d), lambda j, i: (jnp.minimum(i, na - 1), j)),
            pl.BlockSpec((tm, d), lambda j, i: (jnp.maximum(i - na, 0), j))]


def _rmsnorm_kernel(xa_ref, xb_ref, g_ref, o_ref, *, NA_BLOCKS):
    def body(x_ref):
        x = x_ref[...]
        y = x * lax.rsqrt(jnp.mean(x * x, axis=-1, keepdims=True) + NORM_EPS)
        o_ref[...] = (y * g_ref[...]).astype(o_ref.dtype)

    pl.when(pl.program_id(0) < NA_BLOCKS)(lambda: body(xa_ref))
    pl.when(pl.program_id(0) >= NA_BLOCKS)(lambda: body(xb_ref))


def rmsnorm(xa, xb, g, out_dtype, tm=512):
    (na_rows, d), nb_rows = xa.shape, xb.shape[0]
    assert na_rows % tm == 0 and nb_rows % tm == 0
    na = na_rows // tm
    return pl.pallas_call(
        functools.partial(_rmsnorm_kernel, NA_BLOCKS=na),
        grid=((na_rows + nb_rows) // tm,),
        in_specs=_two_source_specs(tm, d, na, 1) + [pl.BlockSpec((1, d), lambda i: (0, 0))],
        out_specs=pl.BlockSpec((tm, d), lambda i: (i, 0)),
        out_shape=jax.ShapeDtypeStruct((na_rows + nb_rows, d), out_dtype),
        compiler_params=_params("arbitrary"),
        name="rmsnorm",
    )(xa, xb, g.reshape(1, d))


def _matmul_kernel(x_ref, w_ref, o_ref):
    o_ref[...] = jnp.dot(x_ref[...], w_ref[...], preferred_element_type=f32).astype(o_ref.dtype)


def matmul(x, w, tn, tm=512, out_dtype=f32):
    n, k = x.shape
    m = w.shape[1]
    return pl.pallas_call(
        _matmul_kernel,
        grid=(m // tn, n // tm),
        in_specs=[pl.BlockSpec((tm, k), lambda j, i: (i, 0)), pl.BlockSpec((k, tn), lambda j, i: (0, j))],
        out_specs=pl.BlockSpec((tm, tn), lambda j, i: (i, j)),
        out_shape=jax.ShapeDtypeStruct((n, m), out_dtype),
        compiler_params=_params("parallel", "parallel"),
        name="matmul",
    )(x, w)


def _cap(t):
    return GATE_CAP * jnp.tanh(t / GATE_CAP)


def _log_sigmoid(x):
    return jnp.minimum(x, 0.0) - jnp.log1p(jnp.exp(-jnp.abs(x)))


def _mlstm_kernel(q_ref, k_ref, v_ref, o_ref, g_ref, bi_ref, bf_ref, mhw_ref, c0_ref, n0_ref, m0_ref,
                  y_ref, c_ref, n_ref, m_ref, *, L, H, DK, DV):
    @pl.when(pl.program_id(1) == 0)
    def _():
        c_ref[...] = c0_ref[...]
        n_ref[...] = n0_ref[...]
        m_ref[...] = m0_ref[...]

    gates = g_ref[...]
    li_all = _cap(gates + bi_ref[...])
    lf_all = _log_sigmoid(_cap(gates + bf_ref[...]))
    causal = _iota((L, L), 1) <= _iota((L, L), 0)
    b_all = _dot_hi(causal.astype(f32), lf_all)
    sel = (_iota((8, LANES), 0) == _iota((8, LANES), 1)).astype(f32)
    li_rows = _dot_hi(sel, li_all, NT)
    b_rows = _dot_hi(sel, b_all, NT)

    for h in range(H):
        q = q_ref[:, h * DK:(h + 1) * DK]
        k = k_ref[:, h * DK:(h + 1) * DK] * (DK ** -0.5)
        v = v_ref[:, h * DV:(h + 1) * DV]
        bcol = b_all[:, H + h:H + h + 1]
        licol = li_all[:, h:h + 1]
        brow = b_rows[H + h:H + h + 1, :]
        lirow = li_rows[h:h + 1, :]
        m_prev = m_ref[0, :, h:h + 1]
        log_d = jnp.where(causal, bcol - brow + lirow, -jnp.inf)
        m_inter = m_prev + bcol
        m_t = jnp.maximum(m_inter, jnp.max(log_d, axis=-1, keepdims=True))
        s = _dot(q, k, NT) * jnp.exp(log_d - m_t)
        scale = jnp.exp(m_inter - m_t)
        c_prev = c_ref[0, h]
        n_prev = n_ref[0, h:h + 1, :]
        num = _dot(s, v) + scale * _dot(q, c_prev)
        den = jnp.sum(s, axis=-1, keepdims=True) + scale * jnp.sum(q * n_prev, axis=-1, keepdims=True)
        hh = num / jnp.maximum(jnp.abs(den), jnp.exp(-m_t))
        b_end = bcol[L - 1:L, :]
        g_end = b_end - bcol + licol
        m_new = jnp.maximum(m_prev + b_end, jnp.max(g_end, axis=0, keepdims=True))
        wts = jnp.exp(g_end - m_new)
        dec = jnp.exp(m_prev + b_end - m_new)
        kw = k * wts
        c_ref[0, h] = dec * c_prev + _dot(kw, v, TN)
        n_ref[0, h:h + 1, :] = dec * n_prev + jnp.sum(kw, axis=0, keepdims=True)
        m_ref[0, :, h:h + 1] = m_new
        hn = hh * lax.rsqrt(jnp.mean(hh * hh, axis=-1, keepdims=True) + MH_EPS)
        gate_o = jax.nn.sigmoid(o_ref[:, h * DV:(h + 1) * DV])
        y_ref[:, h * DV:(h + 1) * DV] = (hn * mhw_ref[:, h * DV:(h + 1) * DV] * gate_o).astype(y_ref.dtype)


def mlstm(zb, row0, B, T, L, bias_i, bias_f, mh_w, c0, n0, m0):
    _, H, DK, DV = c0.shape
    assert 2 * H <= 8 and T % L == 0 and row0 % L == 0 and DV == 2 * DK
    nc = T // L
    r0 = row0 // L
    rows = lambda b, c: r0 + b * nc + c
    kern = functools.partial(_mlstm_kernel, L=L, H=H, DK=DK, DV=DV)
    qk_w, v_w = H * DK, H * DV
    gate_blk = (2 * qk_w + 2 * v_w) // LANES
    return pl.pallas_call(
        kern,
        grid=(B, nc),
        in_specs=[
            pl.BlockSpec((L, qk_w), lambda b, c: (rows(b, c), 0)),
            pl.BlockSpec((L, qk_w), lambda b, c: (rows(b, c), 1)),
            pl.BlockSpec((L, v_w), lambda b, c: (rows(b, c), 1)),
            pl.BlockSpec((L, v_w), lambda b, c: (rows(b, c), 2)),
            pl.BlockSpec((L, LANES), lambda b, c: (rows(b, c), gate_blk)),
            pl.BlockSpec((1, LANES), lambda b, c: (0, 0)),
            pl.BlockSpec((1, LANES), lambda b, c: (0, 0)),
            pl.BlockSpec((1, v_w), lambda b, c: (0, 0)),
            pl.BlockSpec((1, H, DK, DV), lambda b, c: (b, 0, 0, 0)),
            pl.BlockSpec((1, H, DK), lambda b, c: (b, 0, 0)),
            pl.BlockSpec((1, 1, H), lambda b, c: (b, 0, 0)),
        ],
        out_specs=[
            pl.BlockSpec((L, v_w), lambda b, c: (b * nc + c, 0)),
            pl.BlockSpec((1, H, DK, DV), lambda b, c: (b, 0, 0, 0)),
            pl.BlockSpec((1, H, DK), lambda b, c: (b, 0, 0)),
            pl.BlockSpec((1, 1, H), lambda b, c: (b, 0, 0)),
        ],
        out_shape=[
            jax.ShapeDtypeStruct((B * T, v_w), bf16),
            jax.ShapeDtypeStruct((B, H, DK, DV), f32),
            jax.ShapeDtypeStruct((B, H, DK), f32),
            jax.ShapeDtypeStruct((B, 1, H), f32),
        ],
        compiler_params=_params("parallel", "arbitrary"),
        name="mlstm",
    )(zb, zb, zb, zb, zb, bias_i, bias_f, mh_w, c0, n0, m0.reshape(B, 1, H))


def _softplus(x):
    return jnp.maximum(x, 0.0) + jnp.log1p(jnp.exp(-jnp.abs(x)))


def _dot_sel(x, sel):
    xh, xl = _split(x)
    s = sel.astype(bf16)
    return jnp.dot(xh, s, preferred_element_type=f32) + jnp.dot(xl, s, preferred_element_type=f32)


def _head_blocks(gw, scale):
    return jnp.where(_iota((gw, gw), 0) // HEAD_A == _iota((gw, gw), 1) // HEAD_A, scale, 0.0).astype(f32)


def _d1(a, b, dims=NN):
    return lax.dot_general(a, b, dims, preferred_element_type=f32)


def _d3(a, b, dims=NN):
    return _d1(a[0], b[0], dims) + _d1(a[0], b[1], dims) + _d1(a[1], b[0], dims)


def _rows(x, sl):
    return tuple(t[sl] for t in x)


def _cat(xs):
    return tuple(jnp.concatenate(ts, axis=0) for ts in zip(*xs))


def _rwkv_kernel(r_ref, k_ref, v_ref, l_ref, sp_ref, mu_ref, w0_ref, a0_ref, kk_ref, ka_ref, rk_ref,
                 lw_ref, lb_ref, wup_ref, aup_ref, gup_ref, s0_ref,
                 o_ref, s_ref, sh_ref,
                 last_r, last_k, last_v, last_l, at_sc, bt_sc, kt_sc, rt_sc, bh_sc, kh_sc, v_sc, gam_sc,
                 y_sc, g_sc, bonus_sc, h_sc, *, L, S, U, DA, WP, AP, TLW):
    NP = DA // LANES
    GW = min(2 * LANES, DA)
    G = HEAD_A // L
    lane = _iota((1, LANES), 1)
    m0 = (lane < HEAD_A).astype(f32)
    m1 = 1.0 - m0

    @pl.when(pl.program_id(1) == 0)
    def _():
        last_r[...] = sp_ref[:, 0, 0:DA]
        last_k[...] = sp_ref[:, 0, DA:2 * DA]
        last_v[...] = sp_ref[:, 0, 2 * DA:3 * DA]
        last_l[...] = sp_ref[:, 0, 3 * DA:3 * DA + TLW]

        def load_state(i, carry):
            s, p = i // NP, i % NP
            hc = s0_ref[s, p]
            h_sc[s, p] = jnp.concatenate([hc * m0, hc * m1], axis=0)
            return carry

        lax.fori_loop(0, S * NP, load_state, 0)

    def shifted(ref, last, mu):
        pieces = []
        for s in range(S):
            cur = ref[s * L:(s + 1) * L, :]
            prev = jnp.where(_iota(cur.shape, 0) == 0, last[s:s + 1, :], pltpu.roll(cur, 1, 0))
            last[s:s + 1, :] = cur[L - 1:L, :]
            pieces.append(cur + mu * (prev - cur))
        return jnp.concatenate(pieces, axis=0)

    r = shifted(r_ref, last_r, mu_ref[:, 0:DA])
    k = shifted(k_ref, last_k, mu_ref[:, DA:2 * DA])
    v = shifted(v_ref, last_v, mu_ref[:, 2 * DA:3 * DA])
    xl = shifted(l_ref, last_l, mu_ref[:, 3 * DA:3 * DA + TLW])
    xw, xa, xg = xl[:, 0:WP], xl[:, WP:WP + AP], xl[:, WP + AP:]

    w_log = -_softplus(-(w0_ref[...] + _dot3(jnp.tanh(xw), wup_ref[...]))) - 0.5
    logw = -jnp.exp(w_log)
    a = jax.nn.sigmoid(a0_ref[...] + _dot3(xa, aup_ref[...]))
    g_sc[...] = _dot(jax.nn.sigmoid(xg), gup_ref[...])

    ones_bd = _head_blocks(GW, 1.0)
    seg_sum = lambda x: jnp.concatenate(
        [_dot_sel(x[:, i * GW:(i + 1) * GW], ones_bd) for i in range(DA // GW)], axis=1)
    kk = k * kk_ref[...]
    kk = kk / jnp.maximum(jnp.sqrt(seg_sum(kk * kk)), 1e-12)
    k_mod = k * (1.0 + (a - 1.0) * ka_ref[...])
    bonus_sc[...] = seg_sum(r * k_mod * rk_ref[...]) * v

    R = S * L
    rr, rc = _iota((R, R), 0), _iota((R, R), 1)
    tri = ((rr // L == rc // L) & (rc <= rr)).astype(f32)
    cs = _dot_hi(tri, logw)
    cs_last = [cs[s * L + L - 1:s * L + L, :] for s in range(S)]
    cs_end = jnp.concatenate([jnp.broadcast_to(t, (L, DA)) for t in cs_last], axis=0)
    e_neg = jnp.exp(-cs)
    e_end = jnp.exp(cs_end - cs)
    bv = kk * a
    vals = (
        (at_sc, -kk * jnp.exp(cs - logw)), (bt_sc, bv * e_neg), (kt_sc, k_mod * e_neg), (rt_sc, r * jnp.exp(cs)),
        (bh_sc, bv * e_end), (kh_sc, k_mod * e_end), (v_sc, v),
    )
    gam = jnp.exp(jnp.concatenate(cs_last, axis=0))
    for p in range(NP):
        sl = slice(p * LANES, (p + 1) * LANES)
        for ref, val in vals:
            ref[p] = val[:, sl]
        gam_sc[p] = gam[:, sl]

    SR = 2 * G * L
    ri = _iota((SR, SR), 0)
    ci = _iota((SR, SR), 1)
    same_blk = ri // L == ci // L
    strict = same_blk & (ci < ri)
    incl = same_blk & (ci <= ri)
    eye = (ri == ci).astype(f32)
    eye_l = _iota((LANES, LANES), 0) == _iota((LANES, LANES), 1)
    n_double = int(math.log2(L)) - 1

    seq_rows = [slice(j * 2 * L, (j + 1) * 2 * L) for j in range(G)]

    def pair_body(i, carry):
        pairs = [i * U + q for q in range(U)]
        chains = [(p, s0) for p in pairs for s0 in range(0, S, G)]
        each = lambda f, *lists: [f(*t) for t in zip(*lists)]
        hsp = [[_split(h_sc[s0 + j, p]) for j in range(G)] for p, s0 in chains]

        def stack(ref):
            out = []
            for p, s0 in chains:
                x = ref[p]
                parts = []
                for s in range(s0, s0 + G):
                    xs = x[s * L:(s + 1) * L]
                    parts += [xs * m0, xs * m1]
                out.append(_split(jnp.concatenate(parts, axis=0)))
            return out

        hi = lambda xs: [x[0] for x in xs]
        cast = lambda xs: [x.astype(bf16) for x in xs]
        la, lr, bt, kt = hi(stack(at_sc)), hi(stack(rt_sc)), hi(stack(bt_sc)), hi(stack(kt_sc))
        vs = stack(v_sc)
        n_ab = each(lambda a, b: jnp.where(strict, _d1(a, b, NT), 0.0), la, bt)
        a_ak = each(lambda a, k: jnp.where(strict, _d1(a, k, NT), 0.0), la, kt)
        xa = each(lambda a, h: jnp.concatenate([_d1(a[sl], h[j][0]) for j, sl in enumerate(seq_rows)], axis=0),
                  la, hsp)
        w = each(lambda x, a, v: x + _d1(a, v[0]), xa, cast(a_ak), vs)
        t_inv = [eye + n for n in n_ab]
        n_pow = cast(n_ab)
        for _ in range(n_double):
            n_pow = cast(each(lambda n: _d1(n, n), n_pow))
            t_inv = each(lambda t, n, tb: t + _d1(n, tb), t_inv, n_pow, cast(t_inv))
        u = each(lambda t, ww: _split(_d1(t, ww)), cast(t_inv), cast(w))
        r_b = cast(each(lambda r, b: jnp.where(incl, _d1(r, b, NT), 0.0), lr, bt))
        r_k = cast(each(lambda r, k: jnp.where(incl, _d1(r, k, NT), 0.0), lr, kt))
        xr = each(lambda r, h: jnp.concatenate([_d1(r[sl], h[j][0]) for j, sl in enumerate(seq_rows)], axis=0),
                  lr, hsp)
        y_st = each(lambda x, rb, uu, rk, v: x + _d1(rb, uu[0]) + _d1(rk, v[0]), xr, r_b, u, r_k, vs)
        lbh, lkh = stack(bh_sc), stack(kh_sc)
        h_new = []
        for ci, (p, s0) in enumerate(chains):
            for j, sl in enumerate(seq_rows):
                dg = _split(jnp.where(eye_l, gam_sc[p][s0 + j:s0 + j + 1, :], 0.0))
                lhs = _cat([_rows(lbh[ci], sl), _rows(lkh[ci], sl), dg])
                rhs = _cat([_rows(u[ci], sl), _rows(vs[ci], sl), hsp[ci][j]])
                h_new.append((s0 + j, p, _d3(lhs, rhs, TN)))
        per_pair = S // G
        for q, p in enumerate(pairs):
            ys = []
            for ci in range(q * per_pair, (q + 1) * per_pair):
                ys += [y_st[ci][j * 2 * L:j * 2 * L + L] + y_st[ci][j * 2 * L + L:(j + 1) * 2 * L] for j in range(G)]
            y_sc[p] = jnp.concatenate(ys, axis=0)
        for s, p, h in h_new:
            h_sc[s, p] = h
        return carry

    lax.fori_loop(0, NP // U, pair_body, 0)

    avg_bd = _head_blocks(GW, 1.0 / HEAD_A)
    for i in range(DA // GW):
        sl = slice(i * GW, (i + 1) * GW)
        y = jnp.concatenate([y_sc[i * (GW // LANES) + j] for j in range(GW // LANES)], axis=1)
        d = y - _dot_sel(y, avg_bd)
        yn = d * lax.rsqrt(_dot_sel(d * d, avg_bd) + GN_EPS)
        out = (yn * lw_ref[:, sl] + lb_ref[:, sl] + bonus_sc[:, sl]) * g_sc[:, sl]
        o_ref[:, sl] = out.astype(o_ref.dtype)

    @pl.when(pl.program_id(1) == pl.num_programs(1) - 1)
    def _():
        def store_state(i, carry):
            s, p = i // NP, i % NP
            hbd = h_sc[s, p]
            s_ref[s, p] = hbd[:HEAD_A] + hbd[HEAD_A:]
            return carry

        lax.fori_loop(0, S * NP, store_state, 0)
        sh_ref[:, 0, 0:DA] = last_r[...]
        sh_ref[:, 0, DA:2 * DA] = last_k[...]
        sh_ref[:, 0, 2 * DA:3 * DA] = last_v[...]
        sh_ref[:, 0, 3 * DA:3 * DA + TLW] = last_l[...]


def rwkv(za, row0, B, T, L, S, shift_prev, s0, mu, w0, a0, k_k, k_a, r_k, lnx_w, lnx_b, wup, aup, gup, U=2):
    DA = w0.shape[-1]
    WP, AP = wup.shape[0], aup.shape[0]
    TLW = WP + AP + gup.shape[0]
    NA = 3 * DA + TLW
    R = S * L
    G = HEAD_A // L
    NP = DA // LANES
    assert za.shape[1] == NA and (3 * DA) % TLW == 0 and L & (L - 1) == 0 and 8 <= L <= HEAD_A
    assert T % L == 0 and B % S == 0 and S % G == 0 and NP % U == 0 and (S == 1 or T == L) and row0 % R == 0
    nc = T // L
    r0 = row0 // R
    rows = lambda b, c: r0 + b * nc + c
    kern = functools.partial(_rwkv_kernel, L=L, S=S, U=U, DA=DA, WP=WP, AP=AP, TLW=TLW)
    vec = pl.BlockSpec((1, DA), lambda b, c: (0, 0))
    full = lambda arr: pl.BlockSpec(arr.shape, lambda b, c: (0,) * arr.ndim)
    state = pl.BlockSpec((S, NP, HEAD_A, LANES), lambda b, c: (b, 0, 0, 0))
    pair_sc = pltpu.VMEM((NP, R, LANES), f32)
    return pl.pallas_call(
        kern,
        grid=(B // S, nc),
        in_specs=[
            pl.BlockSpec((R, DA), lambda b, c: (rows(b, c), 0)),
            pl.BlockSpec((R, DA), lambda b, c: (rows(b, c), 1)),
            pl.BlockSpec((R, DA), lambda b, c: (rows(b, c), 2)),
            pl.BlockSpec((R, TLW), lambda b, c: (rows(b, c), 3 * DA // TLW)),
            pl.BlockSpec((S, 1, NA), lambda b, c: (b, 0, 0)),
            pl.BlockSpec((1, NA), lambda b, c: (0, 0)),
            vec, vec, vec, vec, vec, vec, vec,
            full(wup), full(aup), full(gup),
            state,
        ],
        out_specs=[pl.BlockSpec((R, DA), lambda b, c: (b * nc + c, 0)), state,
                   pl.BlockSpec((S, 1, NA), lambda b, c: (b, 0, 0))],
        out_shape=[jax.ShapeDtypeStruct((B * T, DA), bf16), jax.ShapeDtypeStruct(s0.shape, f32),
                   jax.ShapeDtypeStruct((B, 1, NA), f32)],
        scratch_shapes=[
            pltpu.VMEM((S, DA), f32), pltpu.VMEM((S, DA), f32), pltpu.VMEM((S, DA), f32), pltpu.VMEM((S, TLW), f32),
            pair_sc, pair_sc, pair_sc, pair_sc, pair_sc, pair_sc, pair_sc, pltpu.VMEM((NP, S, LANES), f32),
            pair_sc, pltpu.VMEM((R, DA), f32), pltpu.VMEM((R, DA), f32),
            pltpu.VMEM((S, NP, LANES, LANES), f32),
        ],
        compiler_params=_params("arbitrary", "arbitrary"),
        name="rwkv",
    )(za, za, za, za, shift_prev, mu, w0, a0, k_k, k_a, r_k, lnx_w, lnx_b, wup, aup, gup, s0)


def _merge_kernel(ya1_ref, ya2_ref, yb1_ref, yb2_ref, pa_ref, pb_ref, ga_ref, gb_ref, ba_ref, bb_ref, o_ref, *,
                  NA_BLOCKS):
    ga = jax.nn.sigmoid(ga_ref[...] + ba_ref[...])
    gb = jax.nn.sigmoid(gb_ref[...] + bb_ref[...])

    def body(ya_ref, yb_ref):
        pa = jnp.dot(ya_ref[...], pa_ref[...], preferred_element_type=f32)
        pb = jnp.dot(yb_ref[...], pb_ref[...], preferred_element_type=f32)
        o_ref[...] = (ga * pa + gb * pb).astype(o_ref.dtype)

    pl.when(pl.program_id(1) < NA_BLOCKS)(lambda: body(ya1_ref, yb1_ref))
    pl.when(pl.program_id(1) >= NA_BLOCKS)(lambda: body(ya2_ref, yb2_ref))


def merge(ya, yb, p_a, p_b, zg, b_gate, tm=512, tn=512):
    n = ya[0].shape[0] + ya[1].shape[0]
    da, db = ya[0].shape[1], yb[0].shape[1]
    d = p_a.shape[1]
    nj = d // tn
    assert ya[0].shape[0] % tm == 0 and ya[1].shape[0] % tm == 0
    na = ya[0].shape[0] // tm
    first = lambda j, i: (jnp.minimum(i, na - 1), 0)
    second = lambda j, i: (jnp.maximum(i - na, 0), 0)
    return pl.pallas_call(
        functools.partial(_merge_kernel, NA_BLOCKS=na),
        grid=(nj, n // tm),
        in_specs=[
            pl.BlockSpec((tm, da), first),
            pl.BlockSpec((tm, da), second),
            pl.BlockSpec((tm, db), first),
            pl.BlockSpec((tm, db), second),
            pl.BlockSpec((da, tn), lambda j, i: (0, j)),
            pl.BlockSpec((db, tn), lambda j, i: (0, j)),
            pl.BlockSpec((tm, tn), lambda j, i: (i, j)),
            pl.BlockSpec((tm, tn), lambda j, i: (i, nj + j)),
            pl.BlockSpec((1, tn), lambda j, i: (0, j)),
            pl.BlockSpec((1, tn), lambda j, i: (0, nj + j)),
        ],
        out_specs=pl.BlockSpec((tm, tn), lambda j, i: (i, j)),
        out_shape=jax.ShapeDtypeStruct((n, d), bf16),
        compiler_params=_params("arbitrary", "arbitrary"),
        name="merge",
    )(ya[0], ya[1], yb[0], yb[1], p_a, p_b, zg, zg, b_gate, b_gate)


def _outproj_kernel(u_ref, w_ref, xa_ref, xb_ref, o_ref, *, NA_BLOCKS):
    acc = jnp.dot(u_ref[...], w_ref[...], preferred_element_type=f32)

    @pl.when(pl.program_id(1) < NA_BLOCKS)
    def _():
        o_ref[...] = xa_ref[...] + acc

    @pl.when(pl.program_id(1) >= NA_BLOCKS)
    def _():
        o_ref[...] = xb_ref[...] + acc


def outproj(u, w, xa, xb, tm=512, tn=512):
    n, k = u.shape
    d = w.shape[1]
    assert xa.shape[0] % tm == 0 and xa.shape[0] + xb.shape[0] == n
    na = xa.shape[0] // tm
    return pl.pallas_call(
        functools.partial(_outproj_kernel, NA_BLOCKS=na),
        grid=(d // tn, n // tm),
        in_specs=[pl.BlockSpec((tm, k), lambda j, i: (i, 0)), pl.BlockSpec((k, tn), lambda j, i: (0, j))]
        + _two_source_specs(tm, tn, na, 2),
        out_specs=pl.BlockSpec((tm, tn), lambda j, i: (i, j)),
        out_shape=jax.ShapeDtypeStruct((n, d), f32),
        compiler_params=_params("arbitrary", "arbitrary"),
        name="outproj",
    )(u, w, xa, xb)


def _router_kernel(x_ref, g_ref, wr_ref, br_ref, h_ref, idx_ref, gate_ref, *, E):
    x = x_ref[...]
    h = x * lax.rsqrt(jnp.mean(x * x, axis=-1, keepdims=True) + NORM_EPS) * g_ref[...]
    h_ref[...] = _pack_bf16_halves(h)
    logits = _dot_hi(h, wr_ref[...]) + br_ref[...]
    lane = _iota(logits.shape, 1)
    l = jnp.where(lane < E, logits, -jnp.inf)
    vals, idxs = [], []
    for _ in range(TOP_K):
        mx = jnp.max(l, axis=-1, keepdims=True)
        ix = jnp.min(jnp.where(l == mx, lane, LANES), axis=-1, keepdims=True)
        vals.append(mx)
        idxs.append(ix)
        l = jnp.where(lane == ix, -jnp.inf, l)
    es = [jnp.exp(v - vals[0]) for v in vals]
    tot = functools.reduce(lambda a, b: a + b, es)
    gate_out = jnp.zeros(logits.shape, f32)
    idx_out = jnp.zeros(logits.shape, i32)
    for k in range(TOP_K):
        gate_out = jnp.where(lane == k, es[k] / tot, gate_out)
        idx_out = jnp.where(lane == k, idxs[k], idx_out)
    gate_ref[...] = gate_out
    idx_ref[...] = idx_out


def router(x, g, w_router, b_router, tm=256):
    n, d = x.shape
    E = w_router.shape[1]
    wr = jnp.pad(w_router, ((0, 0), (0, LANES - E)))
    br = jnp.pad(b_router, (0, LANES - E)).reshape(1, LANES)
    return pl.pallas_call(
        functools.partial(_router_kernel, E=E),
        grid=(n // tm,),
        in_specs=[
            pl.BlockSpec((tm, d), lambda i: (i, 0)),
            pl.BlockSpec((1, d), lambda i: (0, 0)),
            pl.BlockSpec((d, LANES), lambda i: (0, 0)),
            pl.BlockSpec((1, LANES), lambda i: (0, 0)),
        ],
        out_specs=[
            pl.BlockSpec((tm, d // 2), lambda i: (i, 0)),
            pl.BlockSpec((tm, LANES), lambda i: (i, 0)),
            pl.BlockSpec((tm, LANES), lambda i: (i, 0)),
        ],
        out_shape=[
            jax.ShapeDtypeStruct((n, d // 2), jnp.uint32),
            jax.ShapeDtypeStruct((n, LANES), i32),
            jax.ShapeDtypeStruct((n, LANES), f32),
        ],
        compiler_params=_params("parallel"),
        name="router",
    )(x, g.reshape(1, d), wr, br)


def _route(idx, E, tb):
    n, k = idx.shape
    sel = idx[:, :, None] == jnp.arange(E, dtype=i32)[None, None, :]
    onehot = jnp.sum(sel.astype(i32), axis=1)
    pos = jnp.cumsum(onehot, axis=0) - onehot
    counts = jnp.sum(onehot, axis=0)
    padded = (counts + tb - 1) // tb * tb
    pend = jnp.cumsum(padded)
    pstart = pend - padded
    dest = jnp.sum(jnp.where(sel, (pstart[None, :] + pos)[:, None, :], 0), axis=2).astype(i32)
    nb = n * k // tb + E
    tok = (jnp.arange(nb * tb, dtype=i32) % n).at[dest.reshape(-1)].set(jnp.repeat(jnp.arange(n, dtype=i32), k))
    first_row = jnp.arange(nb, dtype=i32) * tb
    block_e = jnp.minimum(jnp.sum((pend[None, :] <= first_row[:, None]).astype(i32), axis=1), E - 1)
    n_used = (pend[-1] // tb).astype(i32)
    group_end = pend[block_e] // tb
    next_e = jnp.where(group_end < n_used, block_e[jnp.minimum(group_end, nb - 1)], -1).astype(i32)
    rows_valid = jnp.clip((pstart + counts)[block_e] - first_row, 0, tb)
    rows_valid = jnp.where(first_row < pend[-1], rows_valid, 0).astype(i32)
    return dest, tok, block_e, rows_valid, next_e


def _gather_kernel(tok_ref, nv_ref, src_hbm, o_ref, buf, sem, *, RB):
    i = pl.program_id(0)
    nblk = pl.num_programs(0)

    HB = RB // 2

    def rows_to_fetch(blk):
        return jnp.where(nv_ref[blk] > HB, RB, HB)

    def issue_block(blk):
        slot = blk % 2

        def issue(r, carry):
            row = tok_ref[blk * RB + r]
            pltpu.make_async_copy(src_hbm.at[pl.ds(row, 1)], buf.at[slot, pl.ds(r, 1)], sem.at[slot]).start()
            return carry

        lax.fori_loop(0, rows_to_fetch(blk), issue, 0)

    @pl.when(i == 0)
    def _():
        issue_block(0)

    @pl.when((i + 1 < nblk) & (nv_ref[jnp.minimum(i + 1, nblk - 1)] > 0))
    def _():
        issue_block(i + 1)

    slot = i % 2

    def unpack(rows):
        hi, lo = _unpack_bf16_halves(buf[slot, rows, :])
        half = hi.shape[1]
        o_ref[rows, :half] = hi.astype(o_ref.dtype)
        o_ref[rows, half:] = lo.astype(o_ref.dtype)

    @pl.when(nv_ref[i] > HB)
    def _():
        pltpu.make_async_copy(src_hbm.at[pl.ds(0, RB)], buf.at[slot], sem.at[slot]).wait()
        unpack(slice(0, RB))

    @pl.when((nv_ref[i] > 0) & (nv_ref[i] <= HB))
    def _():
        pltpu.make_async_copy(src_hbm.at[pl.ds(0, HB)], buf.at[slot, pl.ds(0, HB)], sem.at[slot]).wait()
        unpack(slice(0, HB))
        o_ref[HB:, :] = jnp.zeros((RB - HB, o_ref.shape[1]), o_ref.dtype)

    @pl.when(nv_ref[i] == 0)
    def _():
        o_ref[...] = jnp.zeros(o_ref.shape, o_ref.dtype)


def gather_rows(src, tok, rows_valid, rb=256):
    p = tok.shape[0]
    d = 2 * src.shape[1]
    return pl.pallas_call(
        functools.partial(_gather_kernel, RB=rb),
        grid_spec=pltpu.PrefetchScalarGridSpec(
            num_scalar_prefetch=2,
            grid=(p // rb,),
            in_specs=[pl.BlockSpec(memory_space=pl.ANY)],
            out_specs=pl.BlockSpec((rb, d), lambda i, tok, nv: (i, 0)),
            scratch_shapes=[pltpu.VMEM((2, rb, d // 2), src.dtype), pltpu.SemaphoreType.DMA((2,))],
        ),
        out_shape=jax.ShapeDtypeStruct((p, d), bf16),
        compiler_params=_params("arbitrary"),
        name="gather_rows",
    )(tok, rows_valid, src)


def _expert_block(be_ref, nu_ref, nx_ref, group_ref, tile_copy, n_tiles, o_ref, compute):
    j, i = pl.program_id(0), pl.program_id(1)
    nj = pl.num_programs(0)
    n_valid = nu_ref[i]
    tb = o_ref.shape[0]

    def fetch(e, jj, slot):
        for t in range(n_tiles):
            tile_copy(e, jj, t, slot).start()

    @pl.when((j == 0) & (i == 0))
    def _():
        group_ref[0] = 0
        fetch(be_ref[0], 0, 0)

    first = (n_valid > 0) & ((i == 0) | (be_ref[i] != be_ref[jnp.maximum(i - 1, 0)]))

    @pl.when(first)
    def _():
        group = group_ref[0]
        slot = group % 2
        for t in range(n_tiles):
            tile_copy(0, 0, t, slot).wait()
        nxt = nx_ref[i]

        @pl.when(nxt >= 0)
        def _():
            fetch(nxt, j, 1 - slot)

        @pl.when((nxt < 0) & (j + 1 < nj))
        def _():
            fetch(be_ref[0], j + 1, 1 - slot)

        group_ref[0] = group + 1
        _on_valid_rows(n_valid, tb, o_ref, functools.partial(compute, fresh=True, slot=slot))

    @pl.when(jnp.logical_not(first))
    def _():
        _on_valid_rows(n_valid, tb, o_ref, functools.partial(compute, fresh=False, slot=0))


def _last_used_block(rows_valid):
    return jnp.maximum(jnp.sum((rows_valid > 0).astype(i32)) - 1, 0).reshape(1)


def _on_valid_rows(n_valid, tb, o_ref, compute):
    half = tb // 2

    def zero(rows):
        o_ref[rows, :] = jnp.zeros((rows.stop - rows.start, o_ref.shape[1]), o_ref.dtype)

    @pl.when(n_valid > half)
    def _():
        compute(slice(0, tb))

    @pl.when((n_valid > 0) & (n_valid <= half))
    def _():
        compute(slice(0, half))
        zero(slice(half, tb))

    @pl.when(n_valid == 0)
    def _():
        zero(slice(0, tb))


def _chunked_dots(x_ref, rows, land, w_bf, n_tiles, fresh, slot):
    if not fresh:
        x = x_ref[rows, :]
        return [jnp.dot(x, w_bf[t], preferred_element_type=f32) for t in range(n_tiles)]
    k_total = x_ref.shape[1]
    chunk = math.gcd(k_total, CAST_CHUNK)
    acc = [None] * n_tiles
    for k0 in range(0, k_total, chunk):
        ks = slice(k0, k0 + chunk)
        xk = x_ref[rows, ks]
        for t in range(n_tiles):
            wk = land[slot, t, ks, :].astype(bf16)
            w_bf[t, ks, :] = wk
            part = jnp.dot(xk, wk, preferred_element_type=f32)
            acc[t] = part if acc[t] is None else acc[t] + part
    return acc


def _expert_up_kernel(be_ref, nu_ref, nx_ref, lv_ref, x_ref, w_hbm, bg_ref, bl_ref, o_ref, land, w_bf, sem, group_ref,
                      *, tf, f):
    def tile_copy(e, jj, t, slot):
        col = pl.multiple_of(t * f + jj * tf, LANES)
        return pltpu.make_async_copy(w_hbm.at[e, :, pl.ds(col, tf)], land.at[slot, t], sem.at[slot, t])

    def compute(rows, fresh, slot):
        gate, lin = _chunked_dots(x_ref, rows, land, w_bf, 2, fresh, slot)
        gate = jnp.minimum(gate + bg_ref[...], SWIGLU_LIMIT)
        lin = jnp.clip(lin + bl_ref[...], -SWIGLU_LIMIT, SWIGLU_LIMIT)
        o_ref[rows, :] = (gate * jax.nn.sigmoid(SWIGLU_ALPHA * gate) * (lin + 1.0)).astype(o_ref.dtype)

    _expert_block(be_ref, nu_ref, nx_ref, group_ref, tile_copy, 2, o_ref, compute)


def expert_up(xg, w1, b1, block_e, rows_valid, next_e, tb, tf=512):
    p, d = xg.shape
    E, _, f2 = w1.shape
    f = f2 // 2
    nb = p // tb
    nj = f // tf
    return pl.pallas_call(
        functools.partial(_expert_up_kernel, tf=tf, f=f),
        grid_spec=pltpu.PrefetchScalarGridSpec(
            num_scalar_prefetch=4,
            grid=(nj, nb),
            in_specs=[
                pl.BlockSpec((tb, d), lambda j, i, be, nu, nx, lv: (jnp.minimum(i, lv[0]), 0)),
                pl.BlockSpec(memory_space=pl.ANY),
                pl.BlockSpec((None, 1, tf), lambda j, i, be, nu, nx, lv: (be[i], 0, j)),
                pl.BlockSpec((None, 1, tf), lambda j, i, be, nu, nx, lv: (be[i], 0, nj + j)),
            ],
            out_specs=pl.BlockSpec((tb, tf), lambda j, i, be, nu, nx, lv: (i, j)),
            scratch_shapes=[pltpu.VMEM((2, 2, d, tf), f32), pltpu.VMEM((2, d, tf), bf16),
                            pltpu.SemaphoreType.DMA((2, 2)), pltpu.SMEM((1,), i32)],
        ),
        out_shape=jax.ShapeDtypeStruct((p, f), bf16),
        compiler_params=_params("arbitrary", "arbitrary"),
        name="expert_up",
    )(block_e, rows_valid, next_e, _last_used_block(rows_valid),xg, w1, b1.reshape(E, 1, f2), b1.reshape(E, 1, f2))


def _expert_down_kernel(be_ref, nu_ref, nx_ref, lv_ref, h_ref, w_hbm, b_ref, o_ref, land, w_bf, sem, group_ref, *, td):
    def tile_copy(e, jj, t, slot):
        col = pl.multiple_of(jj * td, LANES)
        return pltpu.make_async_copy(w_hbm.at[e, :, pl.ds(col, td)], land.at[slot, t], sem.at[slot, t])

    def compute(rows, fresh, slot):
        (y,) = _chunked_dots(h_ref, rows, land, w_bf, 1, fresh, slot)
        o_ref[rows, :] = _pack_bf16_halves(y + b_ref[...])

    _expert_block(be_ref, nu_ref, nx_ref, group_ref, tile_copy, 1, o_ref, compute)


def expert_down(hid, w2, b2, block_e, rows_valid, next_e, tb, td=1024):
    p, f = hid.shape
    E, _, d = w2.shape
    return pl.pallas_call(
        functools.partial(_expert_down_kernel, td=td),
        grid_spec=pltpu.PrefetchScalarGridSpec(
            num_scalar_prefetch=4,
            grid=(d // td, p // tb),
            in_specs=[
                pl.BlockSpec((tb, f), lambda j, i, be, nu, nx, lv: (jnp.minimum(i, lv[0]), 0)),
                pl.BlockSpec(memory_space=pl.ANY),
                pl.BlockSpec((None, 1, td), lambda j, i, be, nu, nx, lv: (be[i], 0, j)),
            ],
            out_specs=pl.BlockSpec((tb, td // 2), lambda j, i, be, nu, nx, lv: (i, j)),
            scratch_shapes=[pltpu.VMEM((2, 1, f, td), f32), pltpu.VMEM((1, f, td), bf16),
                            pltpu.SemaphoreType.DMA((2, 1)), pltpu.SMEM((1,), i32)],
        ),
        out_shape=jax.ShapeDtypeStruct((p, d // 2), jnp.uint32),
        compiler_params=_params("arbitrary", "arbitrary"),
        name="expert_down",
    )(block_e, rows_valid, next_e, _last_used_block(rows_valid),hid, w2, b2.reshape(E, 1, d))


def _combine_kernel(dest_ref, x_ref, gate_ref, nf_ref, y_hbm, oa_ref, ob_ref, buf, sem, *, TM, NA_BLOCKS, TD):
    i = pl.program_id(0)
    nblk = pl.num_programs(0)

    def issue_block(blk):
        slot = blk % 2

        def issue(t, carry):
            for k in range(TOP_K):
                row = dest_ref[(blk * TM + t) * TOP_K + k]
                pltpu.make_async_copy(y_hbm.at[pl.ds(row, 1)], buf.at[slot, k, pl.ds(t, 1)], sem.at[slot]).start()
            return carry

        lax.fori_loop(0, TM, issue, 0)

    @pl.when(i == 0)
    def _():
        issue_block(0)

    @pl.when(i + 1 < nblk)
    def _():
        issue_block(i + 1)

    slot = i % 2
    for k in range(TOP_K):
        pltpu.make_async_copy(y_hbm.at[pl.ds(0, TM)], buf.at[slot, k], sem.at[slot]).wait()
    acc_hi = acc_lo = None
    for k in range(TOP_K):
        hi, lo = _unpack_bf16_halves(buf[slot, k])
        g = gate_ref[:, k:k + 1]
        acc_hi = g * hi if acc_hi is None else acc_hi + g * hi
        acc_lo = g * lo if acc_lo is None else acc_lo + g * lo
    pieces = []
    for j in range(x_ref.shape[1] // TD):
        cols = slice(j * TD // 2, (j + 1) * TD // 2)
        pieces += [acc_hi[:, cols], acc_lo[:, cols]]
    acc = x_ref[...] + jnp.concatenate(pieces, axis=1)
    y = acc * lax.rsqrt(jnp.mean(acc * acc, axis=-1, keepdims=True) + NORM_EPS) * nf_ref[...]

    @pl.when(pl.program_id(0) < NA_BLOCKS)
    def _():
        oa_ref[...] = y

    @pl.when(pl.program_id(0) >= NA_BLOCKS)
    def _():
        ob_ref[...] = y


def combine(x1, gates, norm_final, yb, dest, n_first, td, tm=128):
    n, d = x1.shape
    na = n_first // tm
    assert n_first % tm == 0 and 0 < na < n // tm
    return pl.pallas_call(
        functools.partial(_combine_kernel, TM=tm, NA_BLOCKS=na, TD=td),
        grid_spec=pltpu.PrefetchScalarGridSpec(
            num_scalar_prefetch=1,
            grid=(n // tm,),
            in_specs=[
                pl.BlockSpec((tm, d), lambda i, dest: (i, 0)),
                pl.BlockSpec((tm, LANES), lambda i, dest: (i, 0)),
                pl.BlockSpec((1, d), lambda i, dest: (0, 0)),
                pl.BlockSpec(memory_space=pl.ANY),
            ],
            out_specs=[
                pl.BlockSpec((tm, d), lambda i, dest: (jnp.minimum(i, na - 1), 0)),
                pl.BlockSpec((tm, d), lambda i, dest: (jnp.maximum(i - na, 0), 0)),
            ],
            scratch_shapes=[pltpu.VMEM((2, TOP_K, tm, d // 2), jnp.uint32), pltpu.SemaphoreType.DMA((2,))],
        ),
        out_shape=[jax.ShapeDtypeStruct((n_first, d), f32), jax.ShapeDtypeStruct((n - n_first, d), f32)],
        compiler_params=_params("arbitrary"),
        name="combine",
    )(dest.reshape(-1), x1, gates, norm_final.reshape(1, d), yb)


def moe_and_final_norm(x1, norm_ffn, w_router, b_router, w1, b1, w2, b2, norm_final, n_first, tb, tf=512, td=512,
                       tm_router=256, tm_combine=128):
    h2, idx, gates = router(x1, norm_ffn, w_router, b_router, tm=tm_router)
    dest, tok, block_e, rows_valid, next_e = _route(idx[:, :TOP_K], w_router.shape[1], tb)
    xg = gather_rows(h2, tok, rows_valid, rb=tb)
    hid = expert_up(xg, w1, b1, block_e, rows_valid, next_e, tb, tf=tf)
    yb = expert_down(hid, w2, b2, block_e, rows_valid, next_e, tb, td=td)
    return combine(x1, gates, norm_final, yb, dest, n_first, td, tm=tm_combine)


def _pick_tile(m, cap=1024):
    units = m // LANES
    best = max(u for u in range(1, cap // LANES + 1) if units % u == 0)
    return best * LANES


def _pow2_chunk(t, cap):
    c = 1
    while c * 2 <= cap and t % (c * 2) == 0:
        c *= 2
    return c


def _state_to_pairs(s):
    B, H = s.shape[:2]
    t = jnp.swapaxes(s, 2, 3).reshape(B, H // 2, 2, HEAD_A, HEAD_A)
    return jnp.swapaxes(t, 2, 3).reshape(B, H // 2, HEAD_A, 2 * HEAD_A)


def _pairs_to_state(hc):
    B, NP = hc.shape[:2]
    t = jnp.swapaxes(hc.reshape(B, NP, HEAD_A, 2, HEAD_A), 2, 3)
    return jnp.swapaxes(t.reshape(B, 2 * NP, HEAD_A, HEAD_A), 2, 3)


def _pad_last(x, width):
    return jnp.pad(x, [(0, 0)] * (x.ndim - 1) + [(0, width - x.shape[-1])])


def kernel(x_prompt, x_sample, state_shift, state_rwkv, state_mlstm_c, state_mlstm_n, state_mlstm_m, norm_mix, w_in, mu_shift, w0, w_up, a0, a_up, g_up, k_k, k_a, r_k, lnx_w, lnx_b, b_igate, b_fgate, mh_norm, b_gate, p_a, p_b, w_out, norm_ffn, w_router, b_router, w_mlp1, b_mlp1, w_mlp2, b_mlp2, norm_final):
    assert norm_mix.shape[0] == 1, "single trunk layer"
    Bp, Tp, D = x_prompt.shape
    Bs, Ts, _ = x_sample.shape
    Np, Ns = Bp * Tp, Bs * Ts
    DA, LW, LA, LG = w0.shape[-1], w_up.shape[1], a_up.shape[1], g_up.shape[1]
    HA = r_k.shape[1]
    assert r_k.shape[2] == HEAD_A and HA * HEAD_A == DA and LG % LANES == 0
    _, _, HB, DK, DV = state_mlstm_c.shape
    DQK, DB = HB * DK, HB * DV
    n_shift = 3 * DA + LW + LA + LG
    n_ml = 2 * DQK + 2 * DB + 2 * HB
    WP, AP = _round_up(LW, LANES), _round_up(LA, LANES)

    def pad_shift_cols(t):
        o = 3 * DA
        return jnp.concatenate(
            [t[..., :o], _pad_last(t[..., o:o + LW], WP), _pad_last(t[..., o + LW:o + LW + LA], AP),
             t[..., o + LW + LA:]], axis=-1)

    def unpad_shift_cols(t):
        o = 3 * DA
        return jnp.concatenate([t[..., :o], t[..., o:o + LW], t[..., o + WP:o + WP + LA], t[..., o + WP + AP:]], axis=-1)

    w = w_in[0]
    wa = pad_shift_cols(w[:, :n_shift]).astype(bf16)
    ob = n_shift + 2 * DQK + 2 * DB
    wb = jnp.concatenate([w[:, n_shift:ob], _pad_last(w[:, ob:n_shift + n_ml], LANES)], axis=-1).astype(bf16)
    wg = w[:, n_shift + n_ml:].astype(bf16)

    xp, xs = x_prompt.reshape(Np, D), x_sample.reshape(Ns, D)
    tm = _pow2_chunk(math.gcd(Np, Ns), 512)
    h = rmsnorm(xp, xs, norm_mix[0], bf16, tm=tm)
    za = matmul(h, wa, _pick_tile(wa.shape[1]), tm=tm)
    zb = matmul(h, wb, _pick_tile(wb.shape[1]), tm=tm)
    zg = matmul(h, wg, _pick_tile(wg.shape[1]), tm=tm)

    row = lambda t: t.reshape(1, -1)
    rwkv_params = (row(pad_shift_cols(mu_shift[0])), row(w0[0]), row(a0[0]), row(k_k[0]), row(k_a[0]), row(r_k[0]),
                   row(lnx_w[0]), row(lnx_b[0]), jnp.pad(w_up[0], ((0, WP - LW), (0, 0))),
                   jnp.pad(a_up[0], ((0, AP - LA), (0, 0))), g_up[0])
    NA = wa.shape[1]

    def rwkv_group(row0, B, T, shift_prev, s0):
        L = _pow2_chunk(T, RWKV_CHUNK)
        S, U = (1, 4) if T > L else (HEAD_A // L, 2)
        return rwkv(za, row0, B, T, L, S, shift_prev, s0, *rwkv_params, U=U)

    ya_p, s_p, sh_p = rwkv_group(0, Bp, Tp, jnp.zeros((Bp, 1, NA), f32), jnp.zeros((Bp, HA // 2, HEAD_A, LANES), f32))
    ya_s, s_s, sh_s = rwkv_group(Np, Bs, Ts, pad_shift_cols(state_shift[0])[:, None, :],
                                 _state_to_pairs(state_rwkv[0]))
    s_p, s_s = _pairs_to_state(s_p), _pairs_to_state(s_s)

    bias_i = jnp.pad(b_igate[0], (0, LANES - HB)).reshape(1, LANES)
    bias_f = jnp.pad(b_fgate[0], (HB, LANES - 2 * HB)).reshape(1, LANES)
    mh_w = row(mh_norm[0])
    yb_p, c_p, n_p, m_p = mlstm(zb, 0, Bp, Tp, math.gcd(Tp, MLSTM_CHUNK), bias_i, bias_f, mh_w,
                                jnp.zeros((Bp, HB, DK, DV), f32), jnp.zeros((Bp, HB, DK), f32), jnp.zeros((Bp, HB), f32))
    yb_s, c_s, n_s, m_s = mlstm(zb, Np, Bs, Ts, math.gcd(Ts, MLSTM_CHUNK), bias_i, bias_f, mh_w,
                                state_mlstm_c[0], state_mlstm_n[0], state_mlstm_m[0])

    tn = _pick_tile(D, 512)
    u = merge((ya_p, ya_s), (yb_p, yb_s), p_a[0].astype(bf16), p_b[0].astype(bf16), zg, row(b_gate[0]), tm=tm, tn=tn)
    x1 = outproj(u, w_out[0].astype(bf16), xp, xs, tm=tm, tn=tn)
    y_p, y_s = moe_and_final_norm(x1, norm_ffn[0], w_router[0], b_router[0], w_mlp1[0], b_mlp1[0], w_mlp2[0],
                                  b_mlp2[0], norm_final, Np, tb=EXPERT_ROWS, tf=_pick_tile(w_mlp2.shape[2], 512),
                                  td=_pick_tile(D, 1024),
                                  tm_router=min(tm, 256), tm_combine=min(tm, 128))

    shift_p = unpad_shift_cols(sh_p[:, 0])
    shift_s = unpad_shift_cols(sh_s[:, 0])
    return (y_p.reshape(Bp, Tp, D), y_s.reshape(Bs, Ts, D),
            shift_p[None], s_p[None], c_p[None], n_p[None], m_p.reshape(1, Bp, HB),
            shift_s[None], s_s[None], c_s[None], n_s[None], m_s.reshape(1, Bs, HB))
```

```python
import functools
import math

import jax
import jax.numpy as jnp
from jax import lax
from jax.experimental import pallas as pl
from jax.experimental.pallas import tpu as pltpu

f32 = jnp.float32
bf16 = jnp.bfloat16
i32 = jnp.int32

LANES = 128
HEAD_A = 64
NORM_EPS = 1e-5
GN_EPS = 64e-5
MH_EPS = 1e-6
GATE_CAP = 15.0
TOP_K = 4
SWIGLU_LIMIT = 7.0
SWIGLU_ALPHA = 1.702
RWKV_CHUNK = 64
MLSTM_CHUNK = 128
EXPERT_ROWS = 512
CAST_CHUNK = 512
VMEM_LIMIT = 56 * 1024 * 1024

NN = (((1,), (0,)), ((), ()))
NT = (((1,), (1,)), ((), ()))
TN = (((0,), (0,)), ((), ()))


def _round_up(x, m):
    return (x + m - 1) // m * m


def _mx(x):
    if x.dtype == bf16 or x.shape[0] % 16 != 0:
        return x
    return x.astype(bf16)


def _dot(a, b, dims=NN):
    return lax.dot_general(_mx(a), _mx(b), dims, preferred_element_type=f32)


def _split(x):
    hi = x.astype(bf16)
    lo = (x - hi.astype(f32)).astype(bf16)
    return hi, lo


def _dot_hi(a, b, dims=NN):
    return lax.dot_general(a, b, dims, preferred_element_type=f32, precision=lax.Precision.HIGHEST)


def _pack_bf16_halves(x):
    bits = lax.bitcast_convert_type(x.astype(bf16).astype(f32), jnp.uint32)
    half = x.shape[1] // 2
    return (bits[:, :half] & jnp.uint32(0xFFFF0000)) | (bits[:, half:] >> 16)


def _unpack_bf16_halves(packed):
    hi = lax.bitcast_convert_type(packed & jnp.uint32(0xFFFF0000), f32)
    lo = lax.bitcast_convert_type(packed << 16, f32)
    return hi, lo


def _iota(shape, dim):
    return lax.broadcasted_iota(i32, shape, dim)


def _params(*sem):
    return pltpu.CompilerParams(dimension_semantics=sem, vmem_limit_bytes=VMEM_LIMIT)


def _two_source_specs(tm, d, na, grid_rank):
    if grid_rank == 1:
        return [pl.BlockSpec((tm, d), lambda i: (jnp.minimum(i, na - 1), 0)),
                pl.BlockSpec((tm, d), lambda i: (jnp.maximum(i - na, 0), 0))]
    return [pl.BlockSpec((tm, d), lambda j, i: (jnp.minimum(i, na - 1), j)),
            pl.BlockSpec((tm, d), lambda j, i: (jnp.maximum(i - na, 0), j))]


def _rmsnorm_kernel(xa_ref, xb_ref, g_ref, o_ref, *, NA_BLOCKS):
    def body(x_ref):
        x = x_ref[...]
        y = x * lax.rsqrt(jnp.mean(x * x, axis=-1, keepdims=True) + NORM_EPS)
        o_ref[...] = (y * g_ref[...]).astype(o_ref.dtype)

    pl.when(pl.program_id(0) < NA_BLOCKS)(lambda: body(xa_ref))
    pl.when(pl.program_id(0) >= NA_BLOCKS)(lambda: body(xb_ref))


def rmsnorm(xa, xb, g, out_dtype, tm=512):
    (na_rows, d), nb_rows = xa.shape, xb.shape[0]
    assert na_rows % tm == 0 and nb_rows % tm == 0
    na = na_rows // tm
    return pl.pallas_call(
        functools.partial(_rmsnorm_kernel, NA_BLOCKS=na),
        grid=((na_rows + nb_rows) // tm,),
        in_specs=_two_source_specs(tm, d, na, 1) + [pl.BlockSpec((1, d), lambda i: (0, 0))],
        out_specs=pl.BlockSpec((tm, d), lambda i: (i, 0)),
        out_shape=jax.ShapeDtypeStruct((na_rows + nb_rows, d), out_dtype),
        compiler_params=_params("arbitrary"),
        name="rmsnorm",
    )(xa, xb, g.reshape(1, d))


def _matmul_kernel(x_ref, w_ref, o_ref):
    o_ref[...] = jnp.dot(x_ref[...], w_ref[...], preferred_element_type=f32).astype(o_ref.dtype)


def matmul(x, w, tn, tm=512, out_dtype=f32):
    n, k = x.shape
    m = w.shape[1]
    return pl.pallas_call(
        _matmul_kernel,
        grid=(m // tn, n // tm),
        in_specs=[pl.BlockSpec((tm, k), lambda j, i: (i, 0)), pl.BlockSpec((k, tn), lambda j, i: (0, j))],
        out_specs=pl.BlockSpec((tm, tn), lambda j, i: (i, j)),
        out_shape=jax.ShapeDtypeStruct((n, m), out_dtype),
        compiler_params=_params("parallel", "parallel"),
        name="matmul",
    )(x, w)


def _cap(t):
    return GATE_CAP * jnp.tanh(t / GATE_CAP)


def _log_sigmoid(x):
    return jnp.minimum(x, 0.0) - jnp.log1p(jnp.exp(-jnp.abs(x)))


def _mlstm_kernel(q_ref, k_ref, v_ref, o_ref, g_ref, bi_ref, bf_ref, mhw_ref, c0_ref, n0_ref, m0_ref,
                  y_ref, c_ref, n_ref, m_ref, *, L, H, DK, DV):
    @pl.when(pl.program_id(1) == 0)
    def _():
        c_ref[...] = c0_ref[...]
        n_ref[...] = n0_ref[...]
        m_ref[...] = m0_ref[...]

    gates = g_ref[...]
    li_all = _cap(gates + bi_ref[...])
    lf_all = _log_sigmoid(_cap(gates + bf_ref[...]))
    causal = _iota((L, L), 1) <= _iota((L, L), 0)
    b_all = _dot_hi(causal.astype(f32), lf_all)
    sel = (_iota((8, LANES), 0) == _iota((8, LANES), 1)).astype(f32)
    li_rows = _dot_hi(sel, li_all, NT)
    b_rows = _dot_hi(sel, b_all, NT)

    for h in range(H):
        q = q_ref[:, h * DK:(h + 1) * DK]
        k = k_ref[:, h * DK:(h + 1) * DK] * (DK ** -0.5)
        v = v_ref[:, h * DV:(h + 1) * DV]
        bcol = b_all[:, H + h:H + h + 1]
        licol = li_all[:, h:h + 1]
        brow = b_rows[H + h:H + h + 1, :]
        lirow = li_rows[h:h + 1, :]
        m_prev = m_ref[0, :, h:h + 1]
        log_d = jnp.where(causal, bcol - brow + lirow, -jnp.inf)
        m_inter = m_prev + bcol
        m_t = jnp.maximum(m_inter, jnp.max(log_d, axis=-1, keepdims=True))
        s = _dot(q, k, NT) * jnp.exp(log_d - m_t)
        scale = jnp.exp(m_inter - m_t)
        c_prev = c_ref[0, h]
        n_prev = n_ref[0, h:h + 1, :]
        num = _dot(s, v) + scale * _dot(q, c_prev)
        den = jnp.sum(s, axis=-1, keepdims=True) + scale * jnp.sum(q * n_prev, axis=-1, keepdims=True)
        hh = num / jnp.maximum(jnp.abs(den), jnp.exp(-m_t))
        b_end = bcol[L - 1:L, :]
        g_end = b_end - bcol + licol
        m_new = jnp.maximum(m_prev + b_end, jnp.max(g_end, axis=0, keepdims=True))
        wts = jnp.exp(g_end - m_new)
        dec = jnp.exp(m_prev + b_end - m_new)
        kw = k * wts
        c_ref[0, h] = dec * c_prev + _dot(kw, v, TN)
        n_ref[0, h:h + 1, :] = dec * n_prev + jnp.sum(kw, axis=0, keepdims=True)
        m_ref[0, :, h:h + 1] = m_new
        hn = hh * lax.rsqrt(jnp.mean(hh * hh, axis=-1, keepdims=True) + MH_EPS)
        gate_o = jax.nn.sigmoid(o_ref[:, h * DV:(h + 1) * DV])
        y_ref[:, h * DV:(h + 1) * DV] = (hn * mhw_ref[:, h * DV:(h + 1) * DV] * gate_o).astype(y_ref.dtype)


def mlstm(zb, row0, B, T, L, bias_i, bias_f, mh_w, c0, n0, m0):
    _, H, DK, DV = c0.shape
    assert 2 * H <= 8 and T % L == 0 and row0 % L == 0 and DV == 2 * DK
    nc = T // L
    r0 = row0 // L
    rows = lambda b, c: r0 + b * nc + c
    kern = functools.partial(_mlstm_kernel, L=L, H=H, DK=DK, DV=DV)
    qk_w, v_w = H * DK, H * DV
    gate_blk = (2 * qk_w + 2 * v_w) // LANES
    return pl.pallas_call(
        kern,
        grid=(B, nc),
        in_specs=[
            pl.BlockSpec((L, qk_w), lambda b, c: (rows(b, c), 0)),
            pl.BlockSpec((L, qk_w), lambda b, c: (rows(b, c), 1)),
            pl.BlockSpec((L, v_w), lambda b, c: (rows(b, c), 1)),
            pl.BlockSpec((L, v_w), lambda b, c: (rows(b, c), 2)),
            pl.BlockSpec((L, LANES), lambda b, c: (rows(b, c), gate_blk)),
            pl.BlockSpec((1, LANES), lambda b, c: (0, 0)),
            pl.BlockSpec((1, LANES), lambda b, c: (0, 0)),
            pl.BlockSpec((1, v_w), lambda b, c: (0, 0)),
            pl.BlockSpec((1, H, DK, DV), lambda b, c: (b, 0, 0, 0)),
            pl.BlockSpec((1, H, DK), lambda b, c: (b, 0, 0)),
            pl.BlockSpec((1, 1, H), lambda b, c: (b, 0, 0)),
        ],
        out_specs=[
            pl.BlockSpec((L, v_w), lambda b, c: (b * nc + c, 0)),
            pl.BlockSpec((1, H, DK, DV), lambda b, c: (b, 0, 0, 0)),
            pl.BlockSpec((1, H, DK), lambda b, c: (b, 0, 0)),
            pl.BlockSpec((1, 1, H), lambda b, c: (b, 0, 0)),
        ],
        out_shape=[
            jax.ShapeDtypeStruct((B * T, v_w), bf16),
            jax.ShapeDtypeStruct((B, H, DK, DV), f32),
            jax.ShapeDtypeStruct((B, H, DK), f32),
            jax.ShapeDtypeStruct((B, 1, H), f32),
        ],
        compiler_params=_params("parallel", "arbitrary"),
        name="mlstm",
    )(zb, zb, zb, zb, zb, bias_i, bias_f, mh_w, c0, n0, m0.reshape(B, 1, H))


def _softplus(x):
    return jnp.maximum(x, 0.0) + jnp.log1p(jnp.exp(-jnp.abs(x)))


def _dot_sel(x, sel):
    xh, xl = _split(x)
    s = sel.astype(bf16)
    return jnp.dot(xh, s, preferred_element_type=f32) + jnp.dot(xl, s, preferred_element_type=f32)


def _head_blocks(gw, scale):
    return jnp.where(_iota((gw, gw), 0) // HEAD_A == _iota((gw, gw), 1) // HEAD_A, scale, 0.0).astype(f32)


def _d1(a, b, dims=NN):
    return lax.dot_general(a, b, dims, preferred_element_type=f32)


def _d3(a, b, dims=NN):
    return _d1(a[0], b[0], dims) + _d1(a[0], b[1], dims) + _d1(a[1], b[0], dims)


def _rows(x, sl):
    return tuple(t[sl] for t in x)


def _cat(xs):
    return tuple(jnp.concatenate(ts, axis=0) for ts in zip(*xs))


def _rwkv_kernel(r_ref, k_ref, v_ref, l_ref, sp_ref, mu_ref, w0_ref, a0_ref, kk_ref, ka_ref, rk_ref,
                 lw_ref, lb_ref, wup_ref, aup_ref, gup_ref, s0_ref,
                 o_ref, s_ref, sh_ref,
                 last_r, last_k, last_v, last_l, at_sc, bt_sc, kt_sc, rt_sc, bh_sc, kh_sc, v_sc, gam_sc,
                 y_sc, g_sc, bonus_sc, h_sc, *, L, S, U, DA, WP, AP, TLW):
    NP = DA // LANES
    GW = min(2 * LANES, DA)
    G = HEAD_A // L
    lane = _iota((1, LANES), 1)
    m0 = (lane < HEAD_A).astype(f32)
    m1 = 1.0 - m0

    @pl.when(pl.program_id(1) == 0)
    def _():
        last_r[...] = sp_ref[:, 0, 0:DA]
        last_k[...] = sp_ref[:, 0, DA:2 * DA]
        last_v[...] = sp_ref[:, 0, 2 * DA:3 * DA]
        last_l[...] = sp_ref[:, 0, 3 * DA:3 * DA + TLW]

        def load_state(i, carry):
            s, p = i // NP, i % NP
            hc = s0_ref[s, p]
            h_sc[s, p] = jnp.concatenate([hc * m0, hc * m1], axis=0)
            return carry

        lax.fori_loop(0, S * NP, load_state, 0)

    def shifted(ref, last, mu):
        pieces = []
        for s in range(S):
            cur = ref[s * L:(s + 1) * L, :]
            prev = jnp.where(_iota(cur.shape, 0) == 0, last[s:s + 1, :], pltpu.roll(cur, 1, 0))
            last[s:s + 1, :] = cur[L - 1:L, :]
            pieces.append(cur + mu * (prev - cur))
        return jnp.concatenate(pieces, axis=0)

    r = shifted(r_ref, last_r, mu_ref[:, 0:DA])
    k = shifted(k_ref, last_k, mu_ref[:, DA:2 * DA])
    v = shifted(v_ref, last_v, mu_ref[:, 2 * DA:3 * DA])
    xl = shifted(l_ref, last_l, mu_ref[:, 3 * DA:3 * DA + TLW])
    xw, xa, xg = xl[:, 0:WP], xl[:, WP:WP + AP], xl[:, WP + AP:]

    w_log = -_softplus(-(w0_ref[...] + _dot(jnp.tanh(xw), wup_ref[...]))) - 0.5
    logw = -jnp.exp(w_log)
    a = jax.nn.sigmoid(a0_ref[...] + _dot(xa, aup_ref[...]))
    g_sc[...] = _dot(jax.nn.sigmoid(xg), gup_ref[...])

    ones_bd = _head_blocks(GW, 1.0)
    seg_sum = lambda x: jnp.concatenate(
        [_dot_sel(x[:, i * GW:(i + 1) * GW], ones_bd) for i in range(DA // GW)], axis=1)
    kk = k * kk_ref[...]
    kk = kk / jnp.maximum(jnp.sqrt(seg_sum(kk * kk)), 1e-12)
    k_mod = k * (1.0 + (a - 1.0) * ka_ref[...])
    bonus_sc[...] = seg_sum(r * k_mod * rk_ref[...]) * v

    R = S * L
    rr, rc = _iota((R, R), 0), _iota((R, R), 1)
    tri = ((rr // L == rc // L) & (rc <= rr)).astype(f32)
    cs = _dot_hi(tri, logw)
    cs_last = [cs[s * L + L - 1:s * L + L, :] for s in range(S)]
    cs_end = jnp.concatenate([jnp.broadcast_to(t, (L, DA)) for t in cs_last], axis=0)
    e_neg = jnp.exp(-cs)
    e_end = jnp.exp(cs_end - cs)
    bv = kk * a
    vals = (
        (at_sc, -kk * jnp.exp(cs - logw)), (bt_sc, bv * e_neg), (kt_sc, k_mod * e_neg), (rt_sc, r * jnp.exp(cs)),
        (bh_sc, bv * e_end), (kh_sc, k_mod * e_end), (v_sc, v),
    )
    gam = jnp.exp(jnp.concatenate(cs_last, axis=0))
    for p in range(NP):
        sl = slice(p * LANES, (p + 1) * LANES)
        for ref, val in vals:
            ref[p] = val[:, sl]
        gam_sc[p] = gam[:, sl]

    SR = 2 * G * L
    ri = _iota((SR, SR), 0)
    ci = _iota((SR, SR), 1)
    same_blk = ri // L == ci // L
    strict = same_blk & (ci < ri)
    incl = same_blk & (ci <= ri)
    eye = (ri == ci).astype(f32)
    eye_l = _iota((LANES, LANES), 0) == _iota((LANES, LANES), 1)
    n_double = int(math.log2(L)) - 1

    seq_rows = [slice(j * 2 * L, (j + 1) * 2 * L) for j in range(G)]

    def pair_body(i, carry):
        pairs = [i * U + q for q in range(U)]
        chains = [(p, s0) for p in pairs for s0 in range(0, S, G)]
        each = lambda f, *lists: [f(*t) for t in zip(*lists)]
        hsp = [[_split(h_sc[s0 + j, p]) for j in range(G)] for p, s0 in chains]

        def stack(ref):
            out = []
            for p, s0 in chains:
                x = ref[p]
                parts = []
                for s in range(s0, s0 + G):
                    xs = x[s * L:(s + 1) * L]
                    parts += [xs * m0, xs * m1]
                out.append(_split(jnp.concatenate(parts, axis=0)))
            return out

        hi = lambda xs: [x[0] for x in xs]
        cast = lambda xs: [x.astype(bf16) for x in xs]
        la, lr, bt, kt = hi(stack(at_sc)), hi(stack(rt_sc)), hi(stack(bt_sc)), hi(stack(kt_sc))
        vs = stack(v_sc)
        n_ab = each(lambda a, b: jnp.where(strict, _d1(a, b, NT), 0.0), la, bt)
        a_ak = each(lambda a, k: jnp.where(strict, _d1(a, k, NT), 0.0), la, kt)
        xa = each(lambda a, h: jnp.concatenate([_d1(a[sl], h[j][0]) for j, sl in enumerate(seq_rows)], axis=0),
                  la, hsp)
        w = each(lambda x, a, v: x + _d1(a, v[0]), xa, cast(a_ak), vs)
        t_inv = [eye + n for n in n_ab]
        n_pow = cast(n_ab)
        for _ in range(n_double):
            n_pow = cast(each(lambda n: _d1(n, n), n_pow))
            t_inv = each(lambda t, n, tb: t + _d1(n, tb), t_inv, n_pow, cast(t_inv))
        u = each(lambda t, ww: _split(_d1(t, ww)), cast(t_inv), cast(w))
        r_b = cast(each(lambda r, b: jnp.where(incl, _d1(r, b, NT), 0.0), lr, bt))
        r_k = cast(each(lambda r, k: jnp.where(incl, _d1(r, k, NT), 0.0), lr, kt))
        xr = each(lambda r, h: jnp.concatenate([_d1(r[sl], h[j][0]) for j, sl in enumerate(seq_rows)], axis=0),
                  lr, hsp)
        y_st = each(lambda x, rb, uu, rk, v: x + _d1(rb, uu[0]) + _d1(rk, v[0]), xr, r_b, u, r_k, vs)
        lbh, lkh = stack(bh_sc), stack(kh_sc)
        h_new = []
        for ci, (p, s0) in enumerate(chains):
            for j, sl in enumerate(seq_rows):
                dg = _split(jnp.where(eye_l, gam_sc[p][s0 + j:s0 + j + 1, :], 0.0))
                lhs = _cat([_rows(lbh[ci], sl), _rows(lkh[ci], sl), dg])
                rhs = _cat([_rows(u[ci], sl), _rows(vs[ci], sl), hsp[ci][j]])
                h_new.append((s0 + j, p, _d3(lhs, rhs, TN)))
        per_pair = S // G
        for q, p in enumerate(pairs):
            ys = []
            for ci in range(q * per_pair, (q + 1) * per_pair):
                ys += [y_st[ci][j * 2 * L:j * 2 * L + L] + y_st[ci][j * 2 * L + L:(j + 1) * 2 * L] for j in range(G)]
            y_sc[p] = jnp.concatenate(ys, axis=0)
        for s, p, h in h_new:
            h_sc[s, p] = h
        return carry

    lax.fori_loop(0, NP // U, pair_body, 0)

    avg_bd = _head_blocks(GW, 1.0 / HEAD_A)
    for i in range(DA // GW):
        sl = slice(i * GW, (i + 1) * GW)
        y = jnp.concatenate([y_sc[i * (GW // LANES) + j] for j in range(GW // LANES)], axis=1)
        d = y - _dot_sel(y, avg_bd)
        yn = d * lax.rsqrt(_dot_sel(d * d, avg_bd) + GN_EPS)
        out = (yn * lw_ref[:, sl] + lb_ref[:, sl] + bonus_sc[:, sl]) * g_sc[:, sl]
        o_ref[:, sl] = out.astype(o_ref.dtype)

    @pl.when(pl.program_id(1) == pl.num_programs(1) - 1)
    def _():
        def store_state(i, carry):
            s, p = i // NP, i % NP
            hbd = h_sc[s, p]
            s_ref[s, p] = hbd[:HEAD_A] + hbd[HEAD_A:]
            return carry

        lax.fori_loop(0, S * NP, store_state, 0)
        sh_ref[:, 0, 0:DA] = last_r[...]
        sh_ref[:, 0, DA:2 * DA] = last_k[...]
        sh_ref[:, 0, 2 * DA:3 * DA] = last_v[...]
        sh_ref[:, 0, 3 * DA:3 * DA + TLW] = last_l[...]


def rwkv(za, row0, B, T, L, S, shift_prev, s0, mu, w0, a0, k_k, k_a, r_k, lnx_w, lnx_b, wup, aup, gup, U=2):
    DA = w0.shape[-1]
    WP, AP = wup.shape[0], aup.shape[0]
    TLW = WP + AP + gup.shape[0]
    NA = 3 * DA + TLW
    R = S * L
    G = HEAD_A // L
    NP = DA // LANES
    assert za.shape[1] == NA and (3 * DA) % TLW == 0 and L & (L - 1) == 0 and 8 <= L <= HEAD_A
    assert T % L == 0 and B % S == 0 and S % G == 0 and NP % U == 0 and (S == 1 or T == L) and row0 % R == 0
    nc = T // L
    r0 = row0 // R
    rows = lambda b, c: r0 + b * nc + c
    kern = functools.partial(_rwkv_kernel, L=L, S=S, U=U, DA=DA, WP=WP, AP=AP, TLW=TLW)
    vec = pl.BlockSpec((1, DA), lambda b, c: (0, 0))
    full = lambda arr: pl.BlockSpec(arr.shape, lambda b, c: (0,) * arr.ndim)
    state = pl.BlockSpec((S, NP, HEAD_A, LANES), lambda b, c: (b, 0, 0, 0))
    pair_sc = pltpu.VMEM((NP, R, LANES), f32)
    return pl.pallas_call(
        kern,
        grid=(B // S, nc),
        in_specs=[
            pl.BlockSpec((R, DA), lambda b, c: (rows(b, c), 0)),
            pl.BlockSpec((R, DA), lambda b, c: (rows(b, c), 1)),
            pl.BlockSpec((R, DA), lambda b, c: (rows(b, c), 2)),
            pl.BlockSpec((R, TLW), lambda b, c: (rows(b, c), 3 * DA // TLW)),
            pl.BlockSpec((S, 1, NA), lambda b, c: (b, 0, 0)),
            pl.BlockSpec((1, NA), lambda b, c: (0, 0)),
            vec, vec, vec, vec, vec, vec, vec,
            full(wup), full(aup), full(gup),
            state,
        ],
        out_specs=[pl.BlockSpec((R, DA), lambda b, c: (b * nc + c, 0)), state,
                   pl.BlockSpec((S, 1, NA), lambda b, c: (b, 0, 0))],
        out_shape=[jax.ShapeDtypeStruct((B * T, DA), bf16), jax.ShapeDtypeStruct(s0.shape, f32),
                   jax.ShapeDtypeStruct((B, 1, NA), f32)],
        scratch_shapes=[
            pltpu.VMEM((S, DA), f32), pltpu.VMEM((S, DA), f32), pltpu.VMEM((S, DA), f32), pltpu.VMEM((S, TLW), f32),
            pair_sc, pair_sc, pair_sc, pair_sc, pair_sc, pair_sc, pair_sc, pltpu.VMEM((NP, S, LANES), f32),
            pair_sc, pltpu.VMEM((R, DA), f32), pltpu.VMEM((R, DA), f32),
            pltpu.VMEM((S, NP, LANES, LANES), f32),
        ],
        compiler_params=_params("arbitrary", "arbitrary"),
        name="rwkv",
    )(za, za, za, za, shift_prev, mu, w0, a0, k_k, k_a, r_k, lnx_w, lnx_b, wup, aup, gup, s0)


def _merge_kernel(ya1_ref, ya2_ref, yb1_ref, yb2_ref, pa_ref, pb_ref, ga_ref, gb_ref, ba_ref, bb_ref, o_ref, *,
                  NA_BLOCKS):
    ga = jax.nn.sigmoid(ga_ref[...] + ba_ref[...])
    gb = jax.nn.sigmoid(gb_ref[...] + bb_ref[...])

    def body(ya_ref, yb_ref):
        pa = jnp.dot(ya_ref[...], pa_ref[...], preferred_element_type=f32)
        pb = jnp.dot(yb_ref[...], pb_ref[...], preferred_element_type=f32)
        o_ref[...] = (ga * pa + gb * pb).astype(o_ref.dtype)

    pl.when(pl.program_id(1) < NA_BLOCKS)(lambda: body(ya1_ref, yb1_ref))
    pl.when(pl.program_id(1) >= NA_BLOCKS)(lambda: body(ya2_ref, yb2_ref))


def merge(ya, yb, p_a, p_b, zg, b_gate, tm=512, tn=512):
    n = ya[0].shape[0] + ya[1].shape[0]
    da, db = ya[0].shape[1], yb[0].shape[1]
    d = p_a.shape[1]
    nj = d // tn
    assert ya[0].shape[0] % tm == 0 and ya[1].shape[0] % tm == 0
    na = ya[0].shape[0] // tm
    first = lambda j, i: (jnp.minimum(i, na - 1), 0)
    second = lambda j, i: (jnp.maximum(i - na, 0), 0)
    return pl.pallas_call(
        functools.partial(_merge_kernel, NA_BLOCKS=na),
        grid=(nj, n // tm),
        in_specs=[
            pl.BlockSpec((tm, da), first),
            pl.BlockSpec((tm, da), second),
            pl.BlockSpec((tm, db), first),
            pl.BlockSpec((tm, db), second),
            pl.BlockSpec((da, tn), lambda j, i: (0, j)),
            pl.BlockSpec((db, tn), lambda j, i: (0, j)),
            pl.BlockSpec((tm, tn), lambda j, i: (i, j)),
            pl.BlockSpec((tm, tn), lambda j, i: (i, nj + j)),
            pl.BlockSpec((1, tn), lambda j, i: (0, j)),
            pl.BlockSpec((1, tn), lambda j, i: (0, nj + j)),
        ],
        out_specs=pl.BlockSpec((tm, tn), lambda j, i: (i, j)),
        out_shape=jax.ShapeDtypeStruct((n, d), bf16),
        compiler_params=_params("arbitrary", "arbitrary"),
        name="merge",
    )(ya[0], ya[1], yb[0], yb[1], p_a, p_b, zg, zg, b_gate, b_gate)


def _outproj_kernel(u_ref, w_ref, xa_ref, xb_ref, o_ref, *, NA_BLOCKS):
    acc = jnp.dot(u_ref[...], w_ref[...], preferred_element_type=f32)

    @pl.when(pl.program_id(1) < NA_BLOCKS)
    def _():
        o_ref[...] = xa_ref[...] + acc

    @pl.when(pl.program_id(1) >= NA_BLOCKS)
    def _():
        o_ref[...] = xb_ref[...] + acc


def outproj(u, w, xa, xb, tm=512, tn=512):
    n, k = u.shape
    d = w.shape[1]
    assert xa.shape[0] % tm == 0 and xa.shape[0] + xb.shape[0] == n
    na = xa.shape[0] // tm
    return pl.pallas_call(
        functools.partial(_outproj_kernel, NA_BLOCKS=na),
        grid=(d // tn, n // tm),
        in_specs=[pl.BlockSpec((tm, k), lambda j, i: (i, 0)), pl.BlockSpec((k, tn), lambda j, i: (0, j))]
        + _two_source_specs(tm, tn, na, 2),
        out_specs=pl.BlockSpec((tm, tn), lambda j, i: (i, j)),
        out_shape=jax.ShapeDtypeStruct((n, d), f32),
        compiler_params=_params("arbitrary", "arbitrary"),
        name="outproj",
    )(u, w, xa, xb)


def _router_kernel(x_ref, g_ref, wr_ref, br_ref, h_ref, idx_ref, gate_ref, *, E):
    x = x_ref[...]
    h = x * lax.rsqrt(jnp.mean(x * x, axis=-1, keepdims=True) + NORM_EPS) * g_ref[...]
    h_ref[...] = _pack_bf16_halves(h)
    logits = _dot_hi(h, wr_ref[...]) + br_ref[...]
    lane = _iota(logits.shape, 1)
    l = jnp.where(lane < E, logits, -jnp.inf)
    vals, idxs = [], []
    for _ in range(TOP_K):
        mx = jnp.max(l, axis=-1, keepdims=True)
        ix = jnp.min(jnp.where(l == mx, lane, LANES), axis=-1, keepdims=True)
        vals.append(mx)
        idxs.append(ix)
        l = jnp.where(lane == ix, -jnp.inf, l)
    es = [jnp.exp(v - vals[0]) for v in vals]
    tot = functools.reduce(lambda a, b: a + b, es)
    gate_out = jnp.zeros(logits.shape, f32)
    idx_out = jnp.zeros(logits.shape, i32)
    for k in range(TOP_K):
        gate_out = jnp.where(lane == k, es[k] / tot, gate_out)
        idx_out = jnp.where(lane == k, idxs[k], idx_out)
    gate_ref[...] = gate_out
    idx_ref[...] = idx_out


def router(x, g, w_router, b_router, tm=256):
    n, d = x.shape
    E = w_router.shape[1]
    wr = jnp.pad(w_router, ((0, 0), (0, LANES - E)))
    br = jnp.pad(b_router, (0, LANES - E)).reshape(1, LANES)
    return pl.pallas_call(
        functools.partial(_router_kernel, E=E),
        grid=(n // tm,),
        in_specs=[
            pl.BlockSpec((tm, d), lambda i: (i, 0)),
            pl.BlockSpec((1, d), lambda i: (0, 0)),
            pl.BlockSpec((d, LANES), lambda i: (0, 0)),
            pl.BlockSpec((1, LANES), lambda i: (0, 0)),
        ],
        out_specs=[
            pl.BlockSpec((tm, d // 2), lambda i: (i, 0)),
            pl.BlockSpec((tm, LANES), lambda i: (i, 0)),
            pl.BlockSpec((tm, LANES), lambda i: (i, 0)),
        ],
        out_shape=[
            jax.ShapeDtypeStruct((n, d // 2), jnp.uint32),
            jax.ShapeDtypeStruct((n, LANES), i32),
            jax.ShapeDtypeStruct((n, LANES), f32),
        ],
        compiler_params=_params("parallel"),
        name="router",
    )(x, g.reshape(1, d), wr, br)


def _route(idx, E, tb):
    n, k = idx.shape
    sel = idx[:, :, None] == jnp.arange(E, dtype=i32)[None, None, :]
    onehot = jnp.sum(sel.astype(i32), axis=1)
    pos = jnp.cumsum(onehot, axis=0) - onehot
    counts = jnp.sum(onehot, axis=0)
    padded = (counts + tb - 1) // tb * tb
    pend = jnp.cumsum(padded)
    pstart = pend - padded
    dest = jnp.sum(jnp.where(sel, (pstart[None, :] + pos)[:, None, :], 0), axis=2).astype(i32)
    nb = n * k // tb + E
    tok = (jnp.arange(nb * tb, dtype=i32) % n).at[dest.reshape(-1)].set(jnp.repeat(jnp.arange(n, dtype=i32), k))
    first_row = jnp.arange(nb, dtype=i32) * tb
    block_e = jnp.minimum(jnp.sum((pend[None, :] <= first_row[:, None]).astype(i32), axis=1), E - 1)
    n_used = (pend[-1] // tb).astype(i32)
    group_end = pend[block_e] // tb
    next_e = jnp.where(group_end < n_used, block_e[jnp.minimum(group_end, nb - 1)], -1).astype(i32)
    rows_valid = jnp.clip((pstart + counts)[block_e] - first_row, 0, tb)
    rows_valid = jnp.where(first_row < pend[-1], rows_valid, 0).astype(i32)
    return dest, tok, block_e, rows_valid, next_e


def _gather_kernel(tok_ref, nv_ref, src_hbm, o_ref, buf, sem, *, RB):
    i = pl.program_id(0)
    nblk = pl.num_programs(0)

    HB = RB // 2

    def rows_to_fetch(blk):
        return jnp.where(nv_ref[blk] > HB, RB, HB)

    def issue_block(blk):
        slot = blk % 2

        def issue(r, carry):
            row = tok_ref[blk * RB + r]
            pltpu.make_async_copy(src_hbm.at[pl.ds(row, 1)], buf.at[slot, pl.ds(r, 1)], sem.at[slot]).start()
            return carry

        lax.fori_loop(0, rows_to_fetch(blk), issue, 0)

    @pl.when(i == 0)
    def _():
        issue_block(0)

    @pl.when((i + 1 < nblk) & (nv_ref[jnp.minimum(i + 1, nblk - 1)] > 0))
    def _():
        issue_block(i + 1)

    slot = i % 2

    def unpack(rows):
        hi, lo = _unpack_bf16_halves(buf[slot, rows, :])
        half = hi.shape[1]
        o_ref[rows, :half] = hi.astype(o_ref.dtype)
        o_ref[rows, half:] = lo.astype(o_ref.dtype)

    @pl.when(nv_ref[i] > HB)
    def _():
        pltpu.make_async_copy(src_hbm.at[pl.ds(0, RB)], buf.at[slot], sem.at[slot]).wait()
        unpack(slice(0, RB))

    @pl.when((nv_ref[i] > 0) & (nv_ref[i] <= HB))
    def _():
        pltpu.make_async_copy(src_hbm.at[pl.ds(0, HB)], buf.at[slot, pl.ds(0, HB)], sem.at[slot]).wait()
        unpack(slice(0, HB))
        o_ref[HB:, :] = jnp.zeros((RB - HB, o_ref.shape[1]), o_ref.dtype)

    @pl.when(nv_ref[i] == 0)
    def _():
        o_ref[...] = jnp.zeros(o_ref.shape, o_ref.dtype)


def gather_rows(src, tok, rows_valid, rb=256):
    p = tok.shape[0]
    d = 2 * src.shape[1]
    return pl.pallas_call(
        functools.partial(_gather_kernel, RB=rb),
        grid_spec=pltpu.PrefetchScalarGridSpec(
            num_scalar_prefetch=2,
            grid=(p // rb,),
            in_specs=[pl.BlockSpec(memory_space=pl.ANY)],
            out_specs=pl.BlockSpec((rb, d), lambda i, tok, nv: (i, 0)),
            scratch_shapes=[pltpu.VMEM((2, rb, d // 2), src.dtype), pltpu.SemaphoreType.DMA((2,))],
        ),
        out_shape=jax.ShapeDtypeStruct((p, d), bf16),
        compiler_params=_params("arbitrary"),
        name="gather_rows",
    )(tok, rows_valid, src)


def _expert_block(be_ref, nu_ref, nx_ref, group_ref, tile_copy, n_tiles, o_ref, compute):
    j, i = pl.program_id(0), pl.program_id(1)
    nj = pl.num_programs(0)
    n_valid = nu_ref[i]
    tb = o_ref.shape[0]

    def fetch(e, jj, slot):
        for t in range(n_tiles):
            tile_copy(e, jj, t, slot).start()

    @pl.when((j == 0) & (i == 0))
    def _():
        group_ref[0] = 0
        fetch(be_ref[0], 0, 0)

    first = (n_valid > 0) & ((i == 0) | (be_ref[i] != be_ref[jnp.maximum(i - 1, 0)]))

    @pl.when(first)
    def _():
        group = group_ref[0]
        slot = group % 2
        for t in range(n_tiles):
            tile_copy(0, 0, t, slot).wait()
        nxt = nx_ref[i]

        @pl.when(nxt >= 0)
        def _():
            fetch(nxt, j, 1 - slot)

        @pl.when((nxt < 0) & (j + 1 < nj))
        def _():
            fetch(be_ref[0], j + 1, 1 - slot)

        group_ref[0] = group + 1
        _on_valid_rows(n_valid, tb, o_ref, functools.partial(compute, fresh=True, slot=slot))

    @pl.when(jnp.logical_not(first))
    def _():
        _on_valid_rows(n_valid, tb, o_ref, functools.partial(compute, fresh=False, slot=0))


def _last_used_block(rows_valid):
    return jnp.maximum(jnp.sum((rows_valid > 0).astype(i32)) - 1, 0).reshape(1)


def _on_valid_rows(n_valid, tb, o_ref, compute):
    half = tb // 2

    def zero(rows):
        o_ref[rows, :] = jnp.zeros((rows.stop - rows.start, o_ref.shape[1]), o_ref.dtype)

    @pl.when(n_valid > half)
    def _():
        compute(slice(0, tb))

    @pl.when((n_valid > 0) & (n_valid <= half))
    def _():
        compute(slice(0, half))
        zero(slice(half, tb))

    @pl.when(n_valid == 0)
    def _():
        zero(slice(0, tb))


def _chunked_dots(x_ref, rows, land, w_bf, n_tiles, fresh, slot):
    if not fresh:
        x = x_ref[rows, :]
        return [jnp.dot(x, w_bf[t], preferred_element_type=f32) for t in range(n_tiles)]
    k_total = x_ref.shape[1]
    chunk = math.gcd(k_total, CAST_CHUNK)
    acc = [None] * n_tiles
    for k0 in range(0, k_total, chunk):
        ks = slice(k0, k0 + chunk)
        xk = x_ref[rows, ks]
        for t in range(n_tiles):
            wk = land[slot, t, ks, :].astype(bf16)
            w_bf[t, ks, :] = wk
            part = jnp.dot(xk, wk, preferred_element_type=f32)
            acc[t] = part if acc[t] is None else acc[t] + part
    return acc


def _expert_up_kernel(be_ref, nu_ref, nx_ref, lv_ref, x_ref, w_hbm, bg_ref, bl_ref, o_ref, land, w_bf, sem, group_ref,
                      *, tf, f):
    def tile_copy(e, jj, t, slot):
        col = pl.multiple_of(t * f + jj * tf, LANES)
        return pltpu.make_async_copy(w_hbm.at[e, :, pl.ds(col, tf)], land.at[slot, t], sem.at[slot, t])

    def compute(rows, fresh, slot):
        gate, lin = _chunked_dots(x_ref, rows, land, w_bf, 2, fresh, slot)
        gate = jnp.minimum(gate + bg_ref[...], SWIGLU_LIMIT)
        lin = jnp.clip(lin + bl_ref[...], -SWIGLU_LIMIT, SWIGLU_LIMIT)
        o_ref[rows, :] = (gate * jax.nn.sigmoid(SWIGLU_ALPHA * gate) * (lin + 1.0)).astype(o_ref.dtype)

    _expert_block(be_ref, nu_ref, nx_ref, group_ref, tile_copy, 2, o_ref, compute)


def expert_up(xg, w1, b1, block_e, rows_valid, next_e, tb, tf=512):
    p, d = xg.shape
    E, _, f2 = w1.shape
    f = f2 // 2
    nb = p // tb
    nj = f // tf
    return pl.pallas_call(
        functools.partial(_expert_up_kernel, tf=tf, f=f),
        grid_spec=pltpu.PrefetchScalarGridSpec(
            num_scalar_prefetch=4,
            grid=(nj, nb),
            in_specs=[
                pl.BlockSpec((tb, d), lambda j, i, be, nu, nx, lv: (jnp.minimum(i, lv[0]), 0)),
                pl.BlockSpec(memory_space=pl.ANY),
                pl.BlockSpec((None, 1, tf), lambda j, i, be, nu, nx, lv: (be[i], 0, j)),
                pl.BlockSpec((None, 1, tf), lambda j, i, be, nu, nx, lv: (be[i], 0, nj + j)),
            ],
            out_specs=pl.BlockSpec((tb, tf), lambda j, i, be, nu, nx, lv: (i, j)),
            scratch_shapes=[pltpu.VMEM((2, 2, d, tf), f32), pltpu.VMEM((2, d, tf), bf16),
                            pltpu.SemaphoreType.DMA((2, 2)), pltpu.SMEM((1,), i32)],
        ),
        out_shape=jax.ShapeDtypeStruct((p, f), bf16),
        compiler_params=_params("arbitrary", "arbitrary"),
        name="expert_up",
    )(block_e, rows_valid, next_e, _last_used_block(rows_valid),xg, w1, b1.reshape(E, 1, f2), b1.reshape(E, 1, f2))


def _expert_down_kernel(be_ref, nu_ref, nx_ref, lv_ref, h_ref, w_hbm, b_ref, o_ref, land, w_bf, sem, group_ref, *, td):
    def tile_copy(e, jj, t, slot):
        col = pl.multiple_of(jj * td, LANES)
        return pltpu.make_async_copy(w_hbm.at[e, :, pl.ds(col, td)], land.at[slot, t], sem.at[slot, t])

    def compute(rows, fresh, slot):
        (y,) = _chunked_dots(h_ref, rows, land, w_bf, 1, fresh, slot)
        o_ref[rows, :] = _pack_bf16_halves(y + b_ref[...])

    _expert_block(be_ref, nu_ref, nx_ref, group_ref, tile_copy, 1, o_ref, compute)


def expert_down(hid, w2, b2, block_e, rows_valid, next_e, tb, td=1024):
    p, f = hid.shape
    E, _, d = w2.shape
    return pl.pallas_call(
        functools.partial(_expert_down_kernel, td=td),
        grid_spec=pltpu.PrefetchScalarGridSpec(
            num_scalar_prefetch=4,
            grid=(d // td, p // tb),
            in_specs=[
                pl.BlockSpec((tb, f), lambda j, i, be, nu, nx, lv: (jnp.minimum(i, lv[0]), 0)),
                pl.BlockSpec(memory_space=pl.ANY),
                pl.BlockSpec((None, 1, td), lambda j, i, be, nu, nx, lv: (be[i], 0, j)),
            ],
            out_specs=pl.BlockSpec((tb, td // 2), lambda j, i, be, nu, nx, lv: (i, j)),
            scratch_shapes=[pltpu.VMEM((2, 1, f, td), f32), pltpu.VMEM((1, f, td), bf16),
                            pltpu.SemaphoreType.DMA((2, 1)), pltpu.SMEM((1,), i32)],
        ),
        out_shape=jax.ShapeDtypeStruct((p, d // 2), jnp.uint32),
        compiler_params=_params("arbitrary", "arbitrary"),
        name="expert_down",
    )(block_e, rows_valid, next_e, _last_used_block(rows_valid),hid, w2, b2.reshape(E, 1, d))


def _combine_kernel(dest_ref, x_ref, gate_ref, nf_ref, y_hbm, oa_ref, ob_ref, buf, sem, *, TM, NA_BLOCKS, TD):
    i = pl.program_id(0)
    nblk = pl.num_programs(0)

    def issue_block(blk):
        slot = blk % 2

        def issue(t, carry):
            for k in range(TOP_K):
                row = dest_ref[(blk * TM + t) * TOP_K + k]
                pltpu.make_async_copy(y_hbm.at[pl.ds(row, 1)], buf.at[slot, k, pl.ds(t, 1)], sem.at[slot]).start()
            return carry

        lax.fori_loop(0, TM, issue, 0)

    @pl.when(i == 0)
    def _():
        issue_block(0)

    @pl.when(i + 1 < nblk)
    def _():
        issue_block(i + 1)

    slot = i % 2
    for k in range(TOP_K):
        pltpu.make_async_copy(y_hbm.at[pl.ds(0, TM)], buf.at[slot, k], sem.at[slot]).wait()
    acc_hi = acc_lo = None
    for k in range(TOP_K):
        hi, lo = _unpack_bf16_halves(buf[slot, k])
        g = gate_ref[:, k:k + 1]
        acc_hi = g * hi if acc_hi is None else acc_hi + g * hi
        acc_lo = g * lo if acc_lo is None else acc_lo + g * lo
    pieces = []
    for j in range(x_ref.shape[1] // TD):
        cols = slice(j * TD // 2, (j + 1) * TD // 2)
        pieces += [acc_hi[:, cols], acc_lo[:, cols]]
    acc = x_ref[...] + jnp.concatenate(pieces, axis=1)
    y = acc * lax.rsqrt(jnp.mean(acc * acc, axis=-1, keepdims=True) + NORM_EPS) * nf_ref[...]

    @pl.when(pl.program_id(0) < NA_BLOCKS)
    def _():
        oa_ref[...] = y

    @pl.when(pl.program_id(0) >= NA_BLOCKS)
    def _():
        ob_ref[...] = y


def combine(x1, gates, norm_final, yb, dest, n_first, td, tm=128):
    n, d = x1.shape
    na = n_first // tm
    assert n_first % tm == 0 and 0 < na < n // tm
    return pl.pallas_call(
        functools.partial(_combine_kernel, TM=tm, NA_BLOCKS=na, TD=td),
        grid_spec=pltpu.PrefetchScalarGridSpec(
            num_scalar_prefetch=1,
            grid=(n // tm,),
            in_specs=[
                pl.BlockSpec((tm, d), lambda i, dest: (i, 0)),
                pl.BlockSpec((tm, LANES), lambda i, dest: (i, 0)),
                pl.BlockSpec((1, d), lambda i, dest: (0, 0)),
                pl.BlockSpec(memory_space=pl.ANY),
            ],
            out_specs=[
                pl.BlockSpec((tm, d), lambda i, dest: (jnp.minimum(i, na - 1), 0)),
                pl.BlockSpec((tm, d), lambda i, dest: (jnp.maximum(i - na, 0), 0)),
            ],
            scratch_shapes=[pltpu.VMEM((2, TOP_K, tm, d // 2), jnp.uint32), pltpu.SemaphoreType.DMA((2,))],
        ),
        out_shape=[jax.ShapeDtypeStruct((n_first, d), f32), jax.ShapeDtypeStruct((n - n_first, d), f32)],
        compiler_params=_params("arbitrary"),
        name="combine",
    )(dest.reshape(-1), x1, gates, norm_final.reshape(1, d), yb)


def moe_and_final_norm(x1, norm_ffn, w_router, b_router, w1, b1, w2, b2, norm_final, n_first, tb, tf=512, td=512,
                       tm_router=256, tm_combine=128):
    h2, idx, gates = router(x1, norm_ffn, w_router, b_router, tm=tm_router)
    dest, tok, block_e, rows_valid, next_e = _route(idx[:, :TOP_K], w_router.shape[1], tb)
    xg = gather_rows(h2, tok, rows_valid, rb=tb)
    hid = expert_up(xg, w1, b1, block_e, rows_valid, next_e, tb, tf=tf)
    yb = expert_down(hid, w2, b2, block_e, rows_valid, next_e, tb, td=td)
    return combine(x1, gates, norm_final, yb, dest, n_first, td, tm=tm_combine)


def _pick_tile(m, cap=1024):
    units = m // LANES
    best = max(u for u in range(1, cap // LANES + 1) if units % u == 0)
    return best * LANES


def _pow2_chunk(t, cap):
    c = 1
    while c * 2 <= cap and t % (c * 2) == 0:
        c *= 2
    return c


def _state_to_pairs(s):
    B, H = s.shape[:2]
    t = jnp.swapaxes(s, 2, 3).reshape(B, H // 2, 2, HEAD_A, HEAD_A)
    return jnp.swapaxes(t, 2, 3).reshape(B, H // 2, HEAD_A, 2 * HEAD_A)


def _pairs_to_state(hc):
    B, NP = hc.shape[:2]
    t = jnp.swapaxes(hc.reshape(B, NP, HEAD_A, 2, HEAD_A), 2, 3)
    return jnp.swapaxes(t.reshape(B, 2 * NP, HEAD_A, HEAD_A), 2, 3)


def _pad_last(x, width):
    return jnp.pad(x, [(0, 0)] * (x.ndim - 1) + [(0, width - x.shape[-1])])


def kernel(x_prompt, x_sample, state_shift, state_rwkv, state_mlstm_c, state_mlstm_n, state_mlstm_m, norm_mix, w_in, mu_shift, w0, w_up, a0, a_up, g_up, k_k, k_a, r_k, lnx_w, lnx_b, b_igate, b_fgate, mh_norm, b_gate, p_a, p_b, w_out, norm_ffn, w_router, b_router, w_mlp1, b_mlp1, w_mlp2, b_mlp2, norm_final):
    assert norm_mix.shape[0] == 1, "single trunk layer"
    Bp, Tp, D = x_prompt.shape
    Bs, Ts, _ = x_sample.shape
    Np, Ns = Bp * Tp, Bs * Ts
    DA, LW, LA, LG = w0.shape[-1], w_up.shape[1], a_up.shape[1], g_up.shape[1]
    HA = r_k.shape[1]
    assert r_k.shape[2] == HEAD_A and HA * HEAD_A == DA and LG % LANES == 0
    _, _, HB, DK, DV = state_mlstm_c.shape
    DQK, DB = HB * DK, HB * DV
    n_shift = 3 * DA + LW + LA + LG
    n_ml = 2 * DQK + 2 * DB + 2 * HB
    WP, AP = _round_up(LW, LANES), _round_up(LA, LANES)

    def pad_shift_cols(t):
        o = 3 * DA
        return jnp.concatenate(
            [t[..., :o], _pad_last(t[..., o:o + LW], WP), _pad_last(t[..., o + LW:o + LW + LA], AP),
             t[..., o + LW + LA:]], axis=-1)

    def unpad_shift_cols(t):
        o = 3 * DA
        return jnp.concatenate([t[..., :o], t[..., o:o + LW], t[..., o + WP:o + WP + LA], t[..., o + WP + AP:]], axis=-1)

    w = w_in[0]
    wa = pad_shift_cols(w[:, :n_shift]).astype(bf16)
    ob = n_shift + 2 * DQK + 2 * DB
    wb = jnp.concatenate([w[:, n_shift:ob], _pad_last(w[:, ob:n_shift + n_ml], LANES)], axis=-1).astype(bf16)
    wg = w[:, n_shift + n_ml:].astype(bf16)

    xp, xs = x_prompt.reshape(Np, D), x_sample.reshape(Ns, D)
    tm = _pow2_chunk(math.gcd(Np, Ns), 512)
    h = rmsnorm(xp, xs, norm_mix[0], bf16, tm=tm)
    za = matmul(h, wa, _pick_tile(wa.shape[1]), tm=tm)
    zb = matmul(h, wb, _pick_tile(wb.shape[1]), tm=tm)
    zg = matmul(h, wg, _pick_tile(wg.shape[1]), tm=tm)

    row = lambda t: t.reshape(1, -1)
    rwkv_params = (row(pad_shift_cols(mu_shift[0])), row(w0[0]), row(a0[0]), row(k_k[0]), row(k_a[0]), row(r_k[0]),
                   row(lnx_w[0]), row(lnx_b[0]), jnp.pad(w_up[0], ((0, WP - LW), (0, 0))),
                   jnp.pad(a_up[0], ((0, AP - LA), (0, 0))), g_up[0])
    NA = wa.shape[1]

    def rwkv_group(row0, B, T, shift_prev, s0):
        L = _pow2_chunk(T, RWKV_CHUNK)
        S, U = (1, math.gcd(8, DA // LANES)) if T > L else (HEAD_A // L, 2)
        return rwkv(za, row0, B, T, L, S, shift_prev, s0, *rwkv_params, U=U)

    ya_p, s_p, sh_p = rwkv_group(0, Bp, Tp, jnp.zeros((Bp, 1, NA), f32), jnp.zeros((Bp, HA // 2, HEAD_A, LANES), f32))
    ya_s, s_s, sh_s = rwkv_group(Np, Bs, Ts, pad_shift_cols(state_shift[0])[:, None, :],
                                 _state_to_pairs(state_rwkv[0]))
    s_p, s_s = _pairs_to_state(s_p), _pairs_to_state(s_s)

    bias_i = jnp.pad(b_igate[0], (0, LANES - HB)).reshape(1, LANES)
    bias_f = jnp.pad(b_fgate[0], (HB, LANES - 2 * HB)).reshape(1, LANES)
    mh_w = row(mh_norm[0])
    yb_p, c_p, n_p, m_p = mlstm(zb, 0, Bp, Tp, math.gcd(Tp, MLSTM_CHUNK), bias_i, bias_f, mh_w,
                                jnp.zeros((Bp, HB, DK, DV), f32), jnp.zeros((Bp, HB, DK), f32), jnp.zeros((Bp, HB), f32))
    yb_s, c_s, n_s, m_s = mlstm(zb, Np, Bs, Ts, math.gcd(Ts, MLSTM_CHUNK), bias_i, bias_f, mh_w,
                                state_mlstm_c[0], state_mlstm_n[0], state_mlstm_m[0])

    tn = _pick_tile(D, 512)
    u = merge((ya_p, ya_s), (yb_p, yb_s), p_a[0].astype(bf16), p_b[0].astype(bf16), zg, row(b_gate[0]), tm=tm, tn=tn)
    x1 = outproj(u, w_out[0].astype(bf16), xp, xs, tm=tm, tn=tn)
    y_p, y_s = moe_and_final_norm(x1, norm_ffn[0], w_router[0], b_router[0], w_mlp1[0], b_mlp1[0], w_mlp2[0],
                                  b_mlp2[0], norm_final, Np, tb=EXPERT_ROWS, tf=_pick_tile(w_mlp2.shape[2], 512),
                                  td=_pick_tile(D, 1024),
                                  tm_router=min(tm, 256), tm_combine=min(tm, 128))

    shift_p = unpad_shift_cols(sh_p[:, 0])
    shift_s = unpad_shift_cols(sh_s[:, 0])
    return (y_p.reshape(Bp, Tp, D), y_s.reshape(Bs, Ts, D),
            shift_p[None], s_p[None], c_p[None], n_p[None], m_p.reshape(1, Bp, HB),
            shift_s[None], s_s[None], c_s[None], n_s[None], m_s.reshape(1, Bs, HB))
```

```python
import functools
import math

import jax
import jax.numpy as jnp
from jax import lax
from jax.experimental import pallas as pl
from jax.experimental.pallas import tpu as pltpu

f32 = jnp.float32
bf16 = jnp.bfloat16
i32 = jnp.int32

LANES = 128
HEAD_A = 64
NORM_EPS = 1e-5
GN_EPS = 64e-5
MH_EPS = 1e-6
GATE_CAP = 15.0
TOP_K = 4
SWIGLU_LIMIT = 7.0
SWIGLU_ALPHA = 1.702
RWKV_CHUNK = 64
MLSTM_CHUNK = 128
EXPERT_ROWS = 512
CAST_CHUNK = 512
VMEM_LIMIT = 56 * 1024 * 1024

NN = (((1,), (0,)), ((), ()))
NT = (((1,), (1,)), ((), ()))
TN = (((0,), (0,)), ((), ()))


def _round_up(x, m):
    return (x + m - 1) // m * m


def _mx(x):
    if x.dtype == bf16 or x.shape[0] % 16 != 0:
        return x
    return x.astype(bf16)


def _dot(a, b, dims=NN):
    return lax.dot_general(_mx(a), _mx(b), dims, preferred_element_type=f32)


def _split(x):
    hi = x.astype(bf16)
    lo = (x - hi.astype(f32)).astype(bf16)
    return hi, lo


def _dot_hi(a, b, dims=NN):
    return lax.dot_general(a, b, dims, preferred_element_type=f32, precision=lax.Precision.HIGHEST)


def _pack_bf16_halves(x):
    bits = lax.bitcast_convert_type(x.astype(bf16).astype(f32), jnp.uint32)
    half = x.shape[1] // 2
    return (bits[:, :half] & jnp.uint32(0xFFFF0000)) | (bits[:, half:] >> 16)


def _unpack_bf16_halves(packed):
    hi = lax.bitcast_convert_type(packed & jnp.uint32(0xFFFF0000), f32)
    lo = lax.bitcast_convert_type(packed << 16, f32)
    return hi, lo


def _iota(shape, dim):
    return lax.broadcasted_iota(i32, shape, dim)


def _params(*sem):
    return pltpu.CompilerParams(dimension_semantics=sem, vmem_limit_bytes=VMEM_LIMIT)


def _two_source_specs(tm, d, na, grid_rank):
    if grid_rank == 1:
        return [pl.BlockSpec((tm, d), lambda i: (jnp.minimum(i, na - 1), 0)),
                pl.BlockSpec((tm, d), lambda i: (jnp.maximum(i - na, 0), 0))]
    return [pl.BlockSpec((tm, d), lambda j, i: (jnp.minimum(i, na - 1), j)),
            pl.BlockSpec((tm, d), lambda j, i: (jnp.maximum(i - na, 0), j))]


def _rmsnorm_kernel(xa_ref, xb_ref, g_ref, o_ref, *, NA_BLOCKS):
    def body(x_ref):
        x = x_ref[...]
        y = x * lax.rsqrt(jnp.mean(x * x, axis=-1, keepdims=True) + NORM_EPS)
        o_ref[...] = (y * g_ref[...]).astype(o_ref.dtype)

    pl.when(pl.program_id(0) < NA_BLOCKS)(lambda: body(xa_ref))
    pl.when(pl.program_id(0) >= NA_BLOCKS)(lambda: body(xb_ref))


def rmsnorm(xa, xb, g, out_dtype, tm=512):
    (na_rows, d), nb_rows = xa.shape, xb.shape[0]
    assert na_rows % tm == 0 and nb_rows % tm == 0
    na = na_rows // tm
    return pl.pallas_call(
        functools.partial(_rmsnorm_kernel, NA_BLOCKS=na),
        grid=((na_rows + nb_rows) // tm,),
        in_specs=_two_source_specs(tm, d, na, 1) + [pl.BlockSpec((1, d), lambda i: (0, 0))],
        out_specs=pl.BlockSpec((tm, d), lambda i: (i, 0)),
        out_shape=jax.ShapeDtypeStruct((na_rows + nb_rows, d), out_dtype),
        compiler_params=_params("arbitrary"),
        name="rmsnorm",
    )(xa, xb, g.reshape(1, d))


def _matmul_kernel(x_ref, w_ref, o_ref):
    o_ref[...] = jnp.dot(x_ref[...], w_ref[...], preferred_element_type=f32).astype(o_ref.dtype)


def matmul(x, w, tn, tm=512, out_dtype=f32):
    n, k = x.shape
    m = w.shape[1]
    return pl.pallas_call(
        _matmul_kernel,
        grid=(m // tn, n // tm),
        in_specs=[pl.BlockSpec((tm, k), lambda j, i: (i, 0)), pl.BlockSpec((k, tn), lambda j, i: (0, j))],
        out_specs=pl.BlockSpec((tm, tn), lambda j, i: (i, j)),
        out_shape=jax.ShapeDtypeStruct((n, m), out_dtype),
        compiler_params=_params("parallel", "parallel"),
        name="matmul",
    )(x, w)


def _cap(t):
    return GATE_CAP * jnp.tanh(t / GATE_CAP)


def _log_sigmoid(x):
    return jnp.minimum(x, 0.0) - jnp.log1p(jnp.exp(-jnp.abs(x)))


def _mlstm_kernel(q_ref, k_ref, v_ref, o_ref, g_ref, bi_ref, bf_ref, mhw_ref, c0_ref, n0_ref, m0_ref,
                  y_ref, c_ref, n_ref, m_ref, *, L, H, DK, DV):
    @pl.when(pl.program_id(1) == 0)
    def _():
        c_ref[...] = c0_ref[...]
        n_ref[...] = n0_ref[...]
        m_ref[...] = m0_ref[...]

    gates = g_ref[...]
    li_all = _cap(gates + bi_ref[...])
    lf_all = _log_sigmoid(_cap(gates + bf_ref[...]))
    causal = _iota((L, L), 1) <= _iota((L, L), 0)
    b_all = _dot_hi(causal.astype(f32), lf_all)
    sel = (_iota((8, LANES), 0) == _iota((8, LANES), 1)).astype(f32)
    li_rows = _dot_hi(sel, li_all, NT)
    b_rows = _dot_hi(sel, b_all, NT)

    for h in range(H):
        q = q_ref[:, h * DK:(h + 1) * DK]
        k = k_ref[:, h * DK:(h + 1) * DK] * (DK ** -0.5)
        v = v_ref[:, h * DV:(h + 1) * DV]
        bcol = b_all[:, H + h:H + h + 1]
        licol = li_all[:, h:h + 1]
        brow = b_rows[H + h:H + h + 1, :]
        lirow = li_rows[h:h + 1, :]
        m_prev = m_ref[0, :, h:h + 1]
        log_d = jnp.where(causal, bcol - brow + lirow, -jnp.inf)
        m_inter = m_prev + bcol
        m_t = jnp.maximum(m_inter, jnp.max(log_d, axis=-1, keepdims=True))
        s = _dot(q, k, NT) * jnp.exp(log_d - m_t)
        scale = jnp.exp(m_inter - m_t)
        c_prev = c_ref[0, h]
        n_prev = n_ref[0, h:h + 1, :]
        num = _dot(s, v) + scale * _dot(q, c_prev)
        den = jnp.sum(s, axis=-1, keepdims=True) + scale * jnp.sum(q * n_prev, axis=-1, keepdims=True)
        hh = num / jnp.maximum(jnp.abs(den), jnp.exp(-m_t))
        b_end = bcol[L - 1:L, :]
        g_end = b_end - bcol + licol
        m_new = jnp.maximum(m_prev + b_end, jnp.max(g_end, axis=0, keepdims=True))
        wts = jnp.exp(g_end - m_new)
        dec = jnp.exp(m_prev + b_end - m_new)
        kw = k * wts
        c_ref[0, h] = dec * c_prev + _dot(kw, v, TN)
        n_ref[0, h:h + 1, :] = dec * n_prev + jnp.sum(kw, axis=0, keepdims=True)
        m_ref[0, :, h:h + 1] = m_new
        hn = hh * lax.rsqrt(jnp.mean(hh * hh, axis=-1, keepdims=True) + MH_EPS)
        gate_o = jax.nn.sigmoid(o_ref[:, h * DV:(h + 1) * DV])
        y_ref[:, h * DV:(h + 1) * DV] = (hn * mhw_ref[:, h * DV:(h + 1) * DV] * gate_o).astype(y_ref.dtype)


def mlstm(zb, row0, B, T, L, bias_i, bias_f, mh_w, c0, n0, m0):
    _, H, DK, DV = c0.shape
    assert 2 * H <= 8 and T % L == 0 and row0 % L == 0 and DV == 2 * DK
    nc = T // L
    r0 = row0 // L
    rows = lambda b, c: r0 + b * nc + c
    kern = functools.partial(_mlstm_kernel, L=L, H=H, DK=DK, DV=DV)
    qk_w, v_w = H * DK, H * DV
    gate_blk = (2 * qk_w + 2 * v_w) // LANES
    return pl.pallas_call(
        kern,
        grid=(B, nc),
        in_specs=[
            pl.BlockSpec((L, qk_w), lambda b, c: (rows(b, c), 0)),
            pl.BlockSpec((L, qk_w), lambda b, c: (rows(b, c), 1)),
            pl.BlockSpec((L, v_w), lambda b, c: (rows(b, c), 1)),
            pl.BlockSpec((L, v_w), lambda b, c: (rows(b, c), 2)),
            pl.BlockSpec((L, LANES), lambda b, c: (rows(b, c), gate_blk)),
            pl.BlockSpec((1, LANES), lambda b, c: (0, 0)),
            pl.BlockSpec((1, LANES), lambda b, c: (0, 0)),
            pl.BlockSpec((1, v_w), lambda b, c: (0, 0)),
            pl.BlockSpec((1, H, DK, DV), lambda b, c: (b, 0, 0, 0)),
            pl.BlockSpec((1, H, DK), lambda b, c: (b, 0, 0)),
            pl.BlockSpec((1, 1, H), lambda b, c: (b, 0, 0)),
        ],
        out_specs=[
            pl.BlockSpec((L, v_w), lambda b, c: (b * nc + c, 0)),
            pl.BlockSpec((1, H, DK, DV), lambda b, c: (b, 0, 0, 0)),
            pl.BlockSpec((1, H, DK), lambda b, c: (b, 0, 0)),
            pl.BlockSpec((1, 1, H), lambda b, c: (b, 0, 0)),
        ],
        out_shape=[
            jax.ShapeDtypeStruct((B * T, v_w), bf16),
            jax.ShapeDtypeStruct((B, H, DK, DV), f32),
            jax.ShapeDtypeStruct((B, H, DK), f32),
            jax.ShapeDtypeStruct((B, 1, H), f32),
        ],
        compiler_params=_params("parallel", "arbitrary"),
        name="mlstm",
    )(zb, zb, zb, zb, zb, bias_i, bias_f, mh_w, c0, n0, m0.reshape(B, 1, H))


def _softplus(x):
    return jnp.maximum(x, 0.0) + jnp.log1p(jnp.exp(-jnp.abs(x)))


def _dot_sel(x, sel):
    xh, xl = _split(x)
    s = sel.astype(bf16)
    return jnp.dot(xh, s, preferred_element_type=f32) + jnp.dot(xl, s, preferred_element_type=f32)


def _head_blocks(gw, scale):
    return jnp.where(_iota((gw, gw), 0) // HEAD_A == _iota((gw, gw), 1) // HEAD_A, scale, 0.0).astype(f32)


def _d1(a, b, dims=NN):
    return lax.dot_general(a, b, dims, preferred_element_type=f32)


def _d3(a, b, dims=NN):
    return _d1(a[0], b[0], dims) + _d1(a[0], b[1], dims) + _d1(a[1], b[0], dims)


def _rows(x, sl):
    return tuple(t[sl] for t in x)


def _cat(xs):
    return tuple(jnp.concatenate(ts, axis=0) for ts in zip(*xs))


def _rwkv_kernel(r_ref, k_ref, v_ref, l_ref, sp_ref, mu_ref, w0_ref, a0_ref, kk_ref, ka_ref, rk_ref,
                 lw_ref, lb_ref, wup_ref, aup_ref, gup_ref, s0_ref,
                 o_ref, s_ref, sh_ref,
                 last_r, last_k, last_v, last_l, at_sc, bt_sc, kt_sc, rt_sc, bh_sc, kh_sc, v_sc, gam_sc,
                 y_sc, g_sc, bonus_sc, h_sc, *, L, S, U, DA, WP, AP, TLW):
    NP = DA // LANES
    GW = min(2 * LANES, DA)
    G = HEAD_A // L
    lane = _iota((1, LANES), 1)
    m0 = (lane < HEAD_A).astype(f32)
    m1 = 1.0 - m0

    @pl.when(pl.program_id(1) == 0)
    def _():
        last_r[...] = sp_ref[:, 0, 0:DA]
        last_k[...] = sp_ref[:, 0, DA:2 * DA]
        last_v[...] = sp_ref[:, 0, 2 * DA:3 * DA]
        last_l[...] = sp_ref[:, 0, 3 * DA:3 * DA + TLW]

        def load_state(i, carry):
            s, p = i // NP, i % NP
            hc = s0_ref[s, p]
            h_sc[s, p] = jnp.concatenate([hc * m0, hc * m1], axis=0)
            return carry

        lax.fori_loop(0, S * NP, load_state, 0)

    def shifted(ref, last, mu):
        pieces = []
        for s in range(S):
            cur = ref[s * L:(s + 1) * L, :]
            prev = jnp.where(_iota(cur.shape, 0) == 0, last[s:s + 1, :], pltpu.roll(cur, 1, 0))
            last[s:s + 1, :] = cur[L - 1:L, :]
            pieces.append(cur + mu * (prev - cur))
        return jnp.concatenate(pieces, axis=0)

    r = shifted(r_ref, last_r, mu_ref[:, 0:DA])
    k = shifted(k_ref, last_k, mu_ref[:, DA:2 * DA])
    v = shifted(v_ref, last_v, mu_ref[:, 2 * DA:3 * DA])
    xl = shifted(l_ref, last_l, mu_ref[:, 3 * DA:3 * DA + TLW])
    xw, xa, xg = xl[:, 0:WP], xl[:, WP:WP + AP], xl[:, WP + AP:]

    w_log = -_softplus(-(w0_ref[...] + _dot(jnp.tanh(xw), wup_ref[...]))) - 0.5
    logw = -jnp.exp(w_log)
    a = jax.nn.sigmoid(a0_ref[...] + _dot(xa, aup_ref[...]))
    g_sc[...] = _dot(jax.nn.sigmoid(xg), gup_ref[...])

    ones_bd = _head_blocks(GW, 1.0)
    seg_sum = lambda x: jnp.concatenate(
        [_dot_sel(x[:, i * GW:(i + 1) * GW], ones_bd) for i in range(DA // GW)], axis=1)
    kk = k * kk_ref[...]
    kk = kk / jnp.maximum(jnp.sqrt(seg_sum(kk * kk)), 1e-12)
    k_mod = k * (1.0 + (a - 1.0) * ka_ref[...])
    bonus_sc[...] = seg_sum(r * k_mod * rk_ref[...]) * v

    R = S * L
    rr, rc = _iota((R, R), 0), _iota((R, R), 1)
    tri = ((rr // L == rc // L) & (rc <= rr)).astype(f32)
    cs = _dot_hi(tri, logw)
    cs_last = [cs[s * L + L - 1:s * L + L, :] for s in range(S)]
    cs_end = jnp.concatenate([jnp.broadcast_to(t, (L, DA)) for t in cs_last], axis=0)
    e_neg = jnp.exp(-cs)
    e_end = jnp.exp(cs_end - cs)
    bv = kk * a
    vals = (
        (at_sc, -kk * jnp.exp(cs - logw)), (bt_sc, bv * e_neg), (kt_sc, k_mod * e_neg), (rt_sc, r * jnp.exp(cs)),
        (bh_sc, bv * e_end), (kh_sc, k_mod * e_end), (v_sc, v),
    )
    gam = jnp.exp(jnp.concatenate(cs_last, axis=0))
    for p in range(NP):
        sl = slice(p * LANES, (p + 1) * LANES)
        for ref, val in vals:
            ref[p] = val[:, sl]
        gam_sc[p] = gam[:, sl]

    SR = 2 * G * L
    ri = _iota((SR, SR), 0)
    ci = _iota((SR, SR), 1)
    same_blk = ri // L == ci // L
    strict = same_blk & (ci < ri)
    incl = same_blk & (ci <= ri)
    eye = (ri == ci).astype(f32)
    eye_l = _iota((LANES, LANES), 0) == _iota((LANES, LANES), 1)
    n_double = int(math.log2(L)) - 1

    seq_rows = [slice(j * 2 * L, (j + 1) * 2 * L) for j in range(G)]

    def pair_body(i, carry):
        pairs = [i * U + q for q in range(U)]
        chains = [(p, s0) for p in pairs for s0 in range(0, S, G)]
        each = lambda f, *lists: [f(*t) for t in zip(*lists)]
        hsp = [[_split(h_sc[s0 + j, p]) for j in range(G)] for p, s0 in chains]

        def stack(ref):
            out = []
            for p, s0 in chains:
                x = ref[p]
                parts = []
                for s in range(s0, s0 + G):
                    xs = x[s * L:(s + 1) * L]
                    parts += [xs * m0, xs * m1]
                out.append(_split(jnp.concatenate(parts, axis=0)))
            return out

        hi = lambda xs: [x[0] for x in xs]
        cast = lambda xs: [x.astype(bf16) for x in xs]
        la, lr, bt, kt = hi(stack(at_sc)), hi(stack(rt_sc)), hi(stack(bt_sc)), hi(stack(kt_sc))
        vs = stack(v_sc)
        n_ab = each(lambda a, b: jnp.where(strict, _d1(a, b, NT), 0.0), la, bt)
        a_ak = each(lambda a, k: jnp.where(strict, _d1(a, k, NT), 0.0), la, kt)
        xa = each(lambda a, h: jnp.concatenate([_d1(a[sl], h[j][0]) for j, sl in enumerate(seq_rows)], axis=0),
                  la, hsp)
        w = each(lambda x, a, v: x + _d1(a, v[0]), xa, cast(a_ak), vs)
        t_inv = [eye + n for n in n_ab]
        n_pow = cast(n_ab)
        for _ in range(n_double):
            n_pow = cast(each(lambda n: _d1(n, n), n_pow))
            t_inv = each(lambda t, n, tb: t + _d1(n, tb), t_inv, n_pow, cast(t_inv))
        u = each(lambda t, ww: _split(_d1(t, ww)), cast(t_inv), cast(w))
        r_b = cast(each(lambda r, b: jnp.where(incl, _d1(r, b, NT), 0.0), lr, bt))
        r_k = cast(each(lambda r, k: jnp.where(incl, _d1(r, k, NT), 0.0), lr, kt))
        xr = each(lambda r, h: jnp.concatenate([_d1(r[sl], h[j][0]) for j, sl in enumerate(seq_rows)], axis=0),
                  lr, hsp)
        y_st = each(lambda x, rb, uu, rk, v: x + _d1(rb, uu[0]) + _d1(rk, v[0]), xr, r_b, u, r_k, vs)
        lbh, lkh = stack(bh_sc), stack(kh_sc)
        h_new = []
        for ci, (p, s0) in enumerate(chains):
            for j, sl in enumerate(seq_rows):
                dg = _split(jnp.where(eye_l, gam_sc[p][s0 + j:s0 + j + 1, :], 0.0))
                lhs = _cat([_rows(lbh[ci], sl), _rows(lkh[ci], sl), dg])
                rhs = _cat([_rows(u[ci], sl), _rows(vs[ci], sl), hsp[ci][j]])
                h_new.append((s0 + j, p, _d3(lhs, rhs, TN)))
        per_pair = S // G
        for q, p in enumerate(pairs):
            ys = []
            for ci in range(q * per_pair, (q + 1) * per_pair):
                ys += [y_st[ci][j * 2 * L:j * 2 * L + L] + y_st[ci][j * 2 * L + L:(j + 1) * 2 * L] for j in range(G)]
            y_sc[p] = jnp.concatenate(ys, axis=0)
        for s, p, h in h_new:
            h_sc[s, p] = h
        return carry

    lax.fori_loop(0, NP // U, pair_body, 0)

    avg_bd = _head_blocks(GW, 1.0 / HEAD_A)
    for i in range(DA // GW):
        sl = slice(i * GW, (i + 1) * GW)
        y = jnp.concatenate([y_sc[i * (GW // LANES) + j] for j in range(GW // LANES)], axis=1)
        d = y - _dot_sel(y, avg_bd)
        yn = d * lax.rsqrt(_dot_sel(d * d, avg_bd) + GN_EPS)
        out = (yn * lw_ref[:, sl] + lb_ref[:, sl] + bonus_sc[:, sl]) * g_sc[:, sl]
        o_ref[:, sl] = out.astype(o_ref.dtype)

    @pl.when(pl.program_id(1) == pl.num_programs(1) - 1)
    def _():
        def store_state(i, carry):
            s, p = i // NP, i % NP
            hbd = h_sc[s, p]
            s_ref[s, p] = hbd[:HEAD_A] + hbd[HEAD_A:]
            return carry

        lax.fori_loop(0, S * NP, store_state, 0)
        sh_ref[:, 0, 0:DA] = last_r[...]
        sh_ref[:, 0, DA:2 * DA] = last_k[...]
        sh_ref[:, 0, 2 * DA:3 * DA] = last_v[...]
        sh_ref[:, 0, 3 * DA:3 * DA + TLW] = last_l[...]


def rwkv(za, row0, B, T, L, S, shift_prev, s0, mu, w0, a0, k_k, k_a, r_k, lnx_w, lnx_b, wup, aup, gup, U=2):
    DA = w0.shape[-1]
    WP, AP = wup.shape[0], aup.shape[0]
    TLW = WP + AP + gup.shape[0]
    NA = 3 * DA + TLW
    R = S * L
    G = HEAD_A // L
    NP = DA // LANES
    assert za.shape[1] == NA and (3 * DA) % TLW == 0 and L & (L - 1) == 0 and 8 <= L <= HEAD_A
    assert T % L == 0 and B % S == 0 and S % G == 0 and NP % U == 0 and (S == 1 or T == L) and row0 % R == 0
    nc = T // L
    r0 = row0 // R
    rows = lambda b, c: r0 + b * nc + c
    kern = functools.partial(_rwkv_kernel, L=L, S=S, U=U, DA=DA, WP=WP, AP=AP, TLW=TLW)
    vec = pl.BlockSpec((1, DA), lambda b, c: (0, 0))
    full = lambda arr: pl.BlockSpec(arr.shape, lambda b, c: (0,) * arr.ndim)
    state = pl.BlockSpec((S, NP, HEAD_A, LANES), lambda b, c: (b, 0, 0, 0))
    pair_sc = pltpu.VMEM((NP, R, LANES), f32)
    return pl.pallas_call(
        kern,
        grid=(B // S, nc),
        in_specs=[
            pl.BlockSpec((R, DA), lambda b, c: (rows(b, c), 0)),
            pl.BlockSpec((R, DA), lambda b, c: (rows(b, c), 1)),
            pl.BlockSpec((R, DA), lambda b, c: (rows(b, c), 2)),
            pl.BlockSpec((R, TLW), lambda b, c: (rows(b, c), 3 * DA // TLW)),
            pl.BlockSpec((S, 1, NA), lambda b, c: (b, 0, 0)),
            pl.BlockSpec((1, NA), lambda b, c: (0, 0)),
            vec, vec, vec, vec, vec, vec, vec,
            full(wup), full(aup), full(gup),
            state,
        ],
        out_specs=[pl.BlockSpec((R, DA), lambda b, c: (b * nc + c, 0)), state,
                   pl.BlockSpec((S, 1, NA), lambda b, c: (b, 0, 0))],
        out_shape=[jax.ShapeDtypeStruct((B * T, DA), bf16), jax.ShapeDtypeStruct(s0.shape, f32),
                   jax.ShapeDtypeStruct((B, 1, NA), f32)],
        scratch_shapes=[
            pltpu.VMEM((S, DA), f32), pltpu.VMEM((S, DA), f32), pltpu.VMEM((S, DA), f32), pltpu.VMEM((S, TLW), f32),
            pair_sc, pair_sc, pair_sc, pair_sc, pair_sc, pair_sc, pair_sc, pltpu.VMEM((NP, S, LANES), f32),
            pair_sc, pltpu.VMEM((R, DA), f32), pltpu.VMEM((R, DA), f32),
            pltpu.VMEM((S, NP, LANES, LANES), f32),
        ],
        compiler_params=_params("arbitrary", "arbitrary"),
        name="rwkv",
    )(za, za, za, za, shift_prev, mu, w0, a0, k_k, k_a, r_k, lnx_w, lnx_b, wup, aup, gup, s0)


def _merge_kernel(ya1_ref, ya2_ref, yb1_ref, yb2_ref, pa_ref, pb_ref, ga_ref, gb_ref, ba_ref, bb_ref, o_ref, *,
                  NA_BLOCKS):
    ga = jax.nn.sigmoid(ga_ref[...] + ba_ref[...])
    gb = jax.nn.sigmoid(gb_ref[...] + bb_ref[...])

    def body(ya_ref, yb_ref):
        pa = jnp.dot(ya_ref[...], pa_ref[...], preferred_element_type=f32)
        pb = jnp.dot(yb_ref[...], pb_ref[...], preferred_element_type=f32)
        o_ref[...] = (ga * pa + gb * pb).astype(o_ref.dtype)

    pl.when(pl.program_id(1) < NA_BLOCKS)(lambda: body(ya1_ref, yb1_ref))
    pl.when(pl.program_id(1) >= NA_BLOCKS)(lambda: body(ya2_ref, yb2_ref))


def merge(ya, yb, p_a, p_b, zg, b_gate, tm=512, tn=512):
    n = ya[0].shape[0] + ya[1].shape[0]
    da, db = ya[0].shape[1], yb[0].shape[1]
    d = p_a.shape[1]
    nj = d // tn
    assert ya[0].shape[0] % tm == 0 and ya[1].shape[0] % tm == 0
    na = ya[0].shape[0] // tm
    first = lambda j, i: (jnp.minimum(i, na - 1), 0)
    second = lambda j, i: (jnp.maximum(i - na, 0), 0)
    return pl.pallas_call(
        functools.partial(_merge_kernel, NA_BLOCKS=na),
        grid=(nj, n // tm),
        in_specs=[
            pl.BlockSpec((tm, da), first),
            pl.BlockSpec((tm, da), second),
            pl.BlockSpec((tm, db), first),
            pl.BlockSpec((tm, db), second),
            pl.BlockSpec((da, tn), lambda j, i: (0, j)),
            pl.BlockSpec((db, tn), lambda j, i: (0, j)),
            pl.BlockSpec((tm, tn), lambda j, i: (i, j)),
            pl.BlockSpec((tm, tn), lambda j, i: (i, nj + j)),
            pl.BlockSpec((1, tn), lambda j, i: (0, j)),
            pl.BlockSpec((1, tn), lambda j, i: (0, nj + j)),
        ],
        out_specs=pl.BlockSpec((tm, tn), lambda j, i: (i, j)),
        out_shape=jax.ShapeDtypeStruct((n, d), bf16),
        compiler_params=_params("arbitrary", "arbitrary"),
        name="merge",
    )(ya[0], ya[1], yb[0], yb[1], p_a, p_b, zg, zg, b_gate, b_gate)


def _outproj_kernel(u_ref, w_ref, xa_ref, xb_ref, o_ref, *, NA_BLOCKS):
    acc = jnp.dot(u_ref[...], w_ref[...], preferred_element_type=f32)

    @pl.when(pl.program_id(1) < NA_BLOCKS)
    def _():
        o_ref[...] = xa_ref[...] + acc

    @pl.when(pl.program_id(1) >= NA_BLOCKS)
    def _():
        o_ref[...] = xb_ref[...] + acc


def outproj(u, w, xa, xb, tm=512, tn=512):
    n, k = u.shape
    d = w.shape[1]
    assert xa.shape[0] % tm == 0 and xa.shape[0] + xb.shape[0] == n
    na = xa.shape[0] // tm
    return pl.pallas_call(
        functools.partial(_outproj_kernel, NA_BLOCKS=na),
        grid=(d // tn, n // tm),
        in_specs=[pl.BlockSpec((tm, k), lambda j, i: (i, 0)), pl.BlockSpec((k, tn), lambda j, i: (0, j))]
        + _two_source_specs(tm, tn, na, 2),
        out_specs=pl.BlockSpec((tm, tn), lambda j, i: (i, j)),
        out_shape=jax.ShapeDtypeStruct((n, d), f32),
        compiler_params=_params("arbitrary", "arbitrary"),
        name="outproj",
    )(u, w, xa, xb)


def _router_kernel(x_ref, g_ref, wr_ref, br_ref, h_ref, idx_ref, gate_ref, *, E):
    x = x_ref[...]
    h = x * lax.rsqrt(jnp.mean(x * x, axis=-1, keepdims=True) + NORM_EPS) * g_ref[...]
    h_ref[...] = _pack_bf16_halves(h)
    logits = _dot_hi(h, wr_ref[...]) + br_ref[...]
    lane = _iota(logits.shape, 1)
    l = jnp.where(lane < E, logits, -jnp.inf)
    vals, idxs = [], []
    for _ in range(TOP_K):
        mx = jnp.max(l, axis=-1, keepdims=True)
        ix = jnp.min(jnp.where(l == mx, lane, LANES), axis=-1, keepdims=True)
        vals.append(mx)
        idxs.append(ix)
        l = jnp.where(lane == ix, -jnp.inf, l)
    es = [jnp.exp(v - vals[0]) for v in vals]
    tot = functools.reduce(lambda a, b: a + b, es)
    gate_out = jnp.zeros(logits.shape, f32)
    idx_out = jnp.zeros(logits.shape, i32)
    for k in range(TOP_K):
        gate_out = jnp.where(lane == k, es[k] / tot, gate_out)
        idx_out = jnp.where(lane == k, idxs[k], idx_out)
    gate_ref[...] = gate_out
    idx_ref[...] = idx_out


def router(x, g, w_router, b_router, tm=256):
    n, d = x.shape
    E = w_router.shape[1]
    wr = jnp.pad(w_router, ((0, 0), (0, LANES - E)))
    br = jnp.pad(b_router, (0, LANES - E)).reshape(1, LANES)
    return pl.pallas_call(
        functools.partial(_router_kernel, E=E),
        grid=(n // tm,),
        in_specs=[
            pl.BlockSpec((tm, d), lambda i: (i, 0)),
            pl.BlockSpec((1, d), lambda i: (0, 0)),
            pl.BlockSpec((d, LANES), lambda i: (0, 0)),
            pl.BlockSpec((1, LANES), lambda i: (0, 0)),
        ],
        out_specs=[
            pl.BlockSpec((tm, d // 2), lambda i: (i, 0)),
            pl.BlockSpec((tm, LANES), lambda i: (i, 0)),
            pl.BlockSpec((tm, LANES), lambda i: (i, 0)),
        ],
        out_shape=[
            jax.ShapeDtypeStruct((n, d // 2), jnp.uint32),
            jax.ShapeDtypeStruct((n, LANES), i32),
            jax.ShapeDtypeStruct((n, LANES), f32),
        ],
        compiler_params=_params("parallel"),
        name="router",
    )(x, g.reshape(1, d), wr, br)


def _route(idx, E, tb):
    n, k = idx.shape
    sel = idx[:, :, None] == jnp.arange(E, dtype=i32)[None, None, :]
    onehot = jnp.sum(sel.astype(i32), axis=1)
    pos = jnp.cumsum(onehot, axis=0) - onehot
    counts = jnp.sum(onehot, axis=0)
    padded = (counts + tb - 1) // tb * tb
    pend = jnp.cumsum(padded)
    pstart = pend - padded
    dest = jnp.sum(jnp.where(sel, (pstart[None, :] + pos)[:, None, :], 0), axis=2).astype(i32)
    nb = n * k // tb + E
    tok = (jnp.arange(nb * tb, dtype=i32) % n).at[dest.reshape(-1)].set(jnp.repeat(jnp.arange(n, dtype=i32), k))
    first_row = jnp.arange(nb, dtype=i32) * tb
    block_e = jnp.minimum(jnp.sum((pend[None, :] <= first_row[:, None]).astype(i32), axis=1), E - 1)
    n_used = (pend[-1] // tb).astype(i32)
    group_end = pend[block_e] // tb
    next_e = jnp.where(group_end < n_used, block_e[jnp.minimum(group_end, nb - 1)], -1).astype(i32)
    rows_valid = jnp.clip((pstart + counts)[block_e] - first_row, 0, tb)
    rows_valid = jnp.where(first_row < pend[-1], rows_valid, 0).astype(i32)
    return dest, tok, block_e, rows_valid, next_e


def _gather_kernel(tok_ref, nv_ref, src_hbm, o_ref, buf, sem, *, RB):
    i = pl.program_id(0)
    nblk = pl.num_programs(0)

    HB = RB // 2

    def rows_to_fetch(blk):
        return jnp.where(nv_ref[blk] > HB, RB, HB)

    def issue_block(blk):
        slot = blk % 2

        def issue(r, carry):
            row = tok_ref[blk * RB + r]
            pltpu.make_async_copy(src_hbm.at[pl.ds(row, 1)], buf.at[slot, pl.ds(r, 1)], sem.at[slot]).start()
            return carry

        lax.fori_loop(0, rows_to_fetch(blk), issue, 0)

    @pl.when(i == 0)
    def _():
        issue_block(0)

    @pl.when((i + 1 < nblk) & (nv_ref[jnp.minimum(i + 1, nblk - 1)] > 0))
    def _():
        issue_block(i + 1)

    slot = i % 2

    def unpack(rows):
        hi, lo = _unpack_bf16_halves(buf[slot, rows, :])
        half = hi.shape[1]
        o_ref[rows, :half] = hi.astype(o_ref.dtype)
        o_ref[rows, half:] = lo.astype(o_ref.dtype)

    @pl.when(nv_ref[i] > HB)
    def _():
        pltpu.make_async_copy(src_hbm.at[pl.ds(0, RB)], buf.at[slot], sem.at[slot]).wait()
        unpack(slice(0, RB))

    @pl.when((nv_ref[i] > 0) & (nv_ref[i] <= HB))
    def _():
        pltpu.make_async_copy(src_hbm.at[pl.ds(0, HB)], buf.at[slot, pl.ds(0, HB)], sem.at[slot]).wait()
        unpack(slice(0, HB))
        o_ref[HB:, :] = jnp.zeros((RB - HB, o_ref.shape[1]), o_ref.dtype)

    @pl.when(nv_ref[i] == 0)
    def _():
        o_ref[...] = jnp.zeros(o_ref.shape, o_ref.dtype)


def gather_rows(src, tok, rows_valid, rb=256):
    p = tok.shape[0]
    d = 2 * src.shape[1]
    return pl.pallas_call(
        functools.partial(_gather_kernel, RB=rb),
        grid_spec=pltpu.PrefetchScalarGridSpec(
            num_scalar_prefetch=2,
            grid=(p // rb,),
            in_specs=[pl.BlockSpec(memory_space=pl.ANY)],
            out_specs=pl.BlockSpec((rb, d), lambda i, tok, nv: (i, 0)),
            scratch_shapes=[pltpu.VMEM((2, rb, d // 2), src.dtype), pltpu.SemaphoreType.DMA((2,))],
        ),
        out_shape=jax.ShapeDtypeStruct((p, d), bf16),
        compiler_params=_params("arbitrary"),
        name="gather_rows",
    )(tok, rows_valid, src)


def _expert_block(be_ref, nu_ref, nx_ref, group_ref, tile_copy, n_tiles, o_ref, compute):
    j, i = pl.program_id(0), pl.program_id(1)
    nj = pl.num_programs(0)
    n_valid = nu_ref[i]
    tb = o_ref.shape[0]

    def fetch(e, jj, slot):
        for t in range(n_tiles):
            tile_copy(e, jj, t, slot).start()

    @pl.when((j == 0) & (i == 0))
    def _():
        group_ref[0] = 0
        fetch(be_ref[0], 0, 0)

    first = (n_valid > 0) & ((i == 0) | (be_ref[i] != be_ref[jnp.maximum(i - 1, 0)]))

    @pl.when(first)
    def _():
        group = group_ref[0]
        slot = group % 2
        for t in range(n_tiles):
            tile_copy(0, 0, t, slot).wait()
        nxt = nx_ref[i]

        @pl.when(nxt >= 0)
        def _():
            fetch(nxt, j, 1 - slot)

        @pl.when((nxt < 0) & (j + 1 < nj))
        def _():
            fetch(be_ref[0], j + 1, 1 - slot)

        group_ref[0] = group + 1
        _on_valid_rows(n_valid, tb, o_ref, functools.partial(compute, fresh=True, slot=slot))

    @pl.when(jnp.logical_not(first))
    def _():
        _on_valid_rows(n_valid, tb, o_ref, functools.partial(compute, fresh=False, slot=0))


def _last_used_block(rows_valid):
    return jnp.maximum(jnp.sum((rows_valid > 0).astype(i32)) - 1, 0).reshape(1)


def _on_valid_rows(n_valid, tb, o_ref, compute):
    half = tb // 2

    def zero(rows):
        o_ref[rows, :] = jnp.zeros((rows.stop - rows.start, o_ref.shape[1]), o_ref.dtype)

    @pl.when(n_valid > half)
    def _():
        compute(slice(0, tb))

    @pl.when((n_valid > 0) & (n_valid <= half))
    def _():
        compute(slice(0, half))
        zero(slice(half, tb))

    @pl.when(n_valid == 0)
    def _():
        zero(slice(0, tb))


def _chunked_dots(x_ref, rows, land, w_bf, n_tiles, fresh, slot):
    if not fresh:
        x = x_ref[rows, :]
        return [jnp.dot(x, w_bf[t], preferred_element_type=f32) for t in range(n_tiles)]
    k_total = x_ref.shape[1]
    chunk = math.gcd(k_total, CAST_CHUNK)
    acc = [None] * n_tiles
    for k0 in range(0, k_total, chunk):
        ks = slice(k0, k0 + chunk)
        xk = x_ref[rows, ks]
        for t in range(n_tiles):
            wk = land[slot, t, ks, :].astype(bf16)
            w_bf[t, ks, :] = wk
            part = jnp.dot(xk, wk, preferred_element_type=f32)
            acc[t] = part if acc[t] is None else acc[t] + part
    return acc


def _expert_up_kernel(be_ref, nu_ref, nx_ref, lv_ref, x_ref, w_hbm, bg_ref, bl_ref, o_ref, land, w_bf, sem, group_ref,
                      *, tf, f):
    def tile_copy(e, jj, t, slot):
        col = pl.multiple_of(t * f + jj * tf, LANES)
        return pltpu.make_async_copy(w_hbm.at[e, :, pl.ds(col, tf)], land.at[slot, t], sem.at[slot, t])

    def compute(rows, fresh, slot):
        gate, lin = _chunked_dots(x_ref, rows, land, w_bf, 2, fresh, slot)
        gate = jnp.minimum(gate + bg_ref[...], SWIGLU_LIMIT)
        lin = jnp.clip(lin + bl_ref[...], -SWIGLU_LIMIT, SWIGLU_LIMIT)
        o_ref[rows, :] = (gate * jax.nn.sigmoid(SWIGLU_ALPHA * gate) * (lin + 1.0)).astype(o_ref.dtype)

    _expert_block(be_ref, nu_ref, nx_ref, group_ref, tile_copy, 2, o_ref, compute)


def expert_up(xg, w1, b1, block_e, rows_valid, next_e, tb, tf=512):
    p, d = xg.shape
    E, _, f2 = w1.shape
    f = f2 // 2
    nb = p // tb
    nj = f // tf
    return pl.pallas_call(
        functools.partial(_expert_up_kernel, tf=tf, f=f),
        grid_spec=pltpu.PrefetchScalarGridSpec(
            num_scalar_prefetch=4,
            grid=(nj, nb),
            in_specs=[
                pl.BlockSpec((tb, d), lambda j, i, be, nu, nx, lv: (jnp.minimum(i, lv[0]), 0)),
                pl.BlockSpec(memory_space=pl.ANY),
                pl.BlockSpec((None, 1, tf), lambda j, i, be, nu, nx, lv: (be[i], 0, j)),
                pl.BlockSpec((None, 1, tf), lambda j, i, be, nu, nx, lv: (be[i], 0, nj + j)),
            ],
            out_specs=pl.BlockSpec((tb, tf), lambda j, i, be, nu, nx, lv: (i, j)),
            scratch_shapes=[pltpu.VMEM((2, 2, d, tf), f32), pltpu.VMEM((2, d, tf), bf16),
                            pltpu.SemaphoreType.DMA((2, 2)), pltpu.SMEM((1,), i32)],
        ),
        out_shape=jax.ShapeDtypeStruct((p, f), bf16),
        compiler_params=_params("arbitrary", "arbitrary"),
        name="expert_up",
    )(block_e, rows_valid, next_e, _last_used_block(rows_valid),xg, w1, b1.reshape(E, 1, f2), b1.reshape(E, 1, f2))


def _expert_down_kernel(be_ref, nu_ref, nx_ref, lv_ref, h_ref, w_hbm, b_ref, o_ref, land, w_bf, sem, group_ref, *, td):
    def tile_copy(e, jj, t, slot):
        col = pl.multiple_of(jj * td, LANES)
        return pltpu.make_async_copy(w_hbm.at[e, :, pl.ds(col, td)], land.at[slot, t], sem.at[slot, t])

    def compute(rows, fresh, slot):
        (y,) = _chunked_dots(h_ref, rows, land, w_bf, 1, fresh, slot)
        o_ref[rows, :] = _pack_bf16_halves(y + b_ref[...])

    _expert_block(be_ref, nu_ref, nx_ref, group_ref, tile_copy, 1, o_ref, compute)


def expert_down(hid, w2, b2, block_e, rows_valid, next_e, tb, td=1024):
    p, f = hid.shape
    E, _, d = w2.shape
    return pl.pallas_call(
        functools.partial(_expert_down_kernel, td=td),
        grid_spec=pltpu.PrefetchScalarGridSpec(
            num_scalar_prefetch=4,
            grid=(d // td, p // tb),
            in_specs=[
                pl.BlockSpec((tb, f), lambda j, i, be, nu, nx, lv: (jnp.minimum(i, lv[0]), 0)),
                pl.BlockSpec(memory_space=pl.ANY),
                pl.BlockSpec((None, 1, td), lambda j, i, be, nu, nx, lv: (be[i], 0, j)),
            ],
            out_specs=pl.BlockSpec((tb, td // 2), lambda j, i, be, nu, nx, lv: (i, j)),
            scratch_shapes=[pltpu.VMEM((2, 1, f, td), f32), pltpu.VMEM((1, f, td), bf16),
                            pltpu.SemaphoreType.DMA((2, 1)), pltpu.SMEM((1,), i32)],
        ),
        out_shape=jax.ShapeDtypeStruct((p, d // 2), jnp.uint32),
        compiler_params=_params("arbitrary", "arbitrary"),
        name="expert_down",
    )(block_e, rows_valid, next_e, _last_used_block(rows_valid),hid, w2, b2.reshape(E, 1, d))


def _combine_kernel(dest_ref, x_ref, gate_ref, nf_ref, y_hbm, oa_ref, ob_ref, buf, sem, *, TM, NA_BLOCKS, TD):
    i = pl.program_id(0)
    nblk = pl.num_programs(0)

    def issue_block(blk):
        slot = blk % 2

        def issue(t, carry):
            for k in range(TOP_K):
                row = dest_ref[(blk * TM + t) * TOP_K + k]
                pltpu.make_async_copy(y_hbm.at[pl.ds(row, 1)], buf.at[slot, k, pl.ds(t, 1)], sem.at[slot]).start()
            return carry

        lax.fori_loop(0, TM, issue, 0)

    @pl.when(i == 0)
    def _():
        issue_block(0)

    @pl.when(i + 1 < nblk)
    def _():
        issue_block(i + 1)

    slot = i % 2
    for k in range(TOP_K):
        pltpu.make_async_copy(y_hbm.at[pl.ds(0, TM)], buf.at[slot, k], sem.at[slot]).wait()
    acc_hi = acc_lo = None
    for k in range(TOP_K):
        hi, lo = _unpack_bf16_halves(buf[slot, k])
        g = gate_ref[:, k:k + 1]
        acc_hi = g * hi if acc_hi is None else acc_hi + g * hi
        acc_lo = g * lo if acc_lo is None else acc_lo + g * lo
    pieces = []
    for j in range(x_ref.shape[1] // TD):
        cols = slice(j * TD // 2, (j + 1) * TD // 2)
        pieces += [acc_hi[:, cols], acc_lo[:, cols]]
    acc = x_ref[...] + jnp.concatenate(pieces, axis=1)
    y = acc * lax.rsqrt(jnp.mean(acc * acc, axis=-1, keepdims=True) + NORM_EPS) * nf_ref[...]

    @pl.when(pl.program_id(0) < NA_BLOCKS)
    def _():
        oa_ref[...] = y

    @pl.when(pl.program_id(0) >= NA_BLOCKS)
    def _():
        ob_ref[...] = y


def combine(x1, gates, norm_final, yb, dest, n_first, td, tm=128):
    n, d = x1.shape
    na = n_first // tm
    assert n_first % tm == 0 and 0 < na < n // tm
    return pl.pallas_call(
        functools.partial(_combine_kernel, TM=tm, NA_BLOCKS=na, TD=td),
        grid_spec=pltpu.PrefetchScalarGridSpec(
            num_scalar_prefetch=1,
            grid=(n // tm,),
            in_specs=[
                pl.BlockSpec((tm, d), lambda i, dest: (i, 0)),
                pl.BlockSpec((tm, LANES), lambda i, dest: (i, 0)),
                pl.BlockSpec((1, d), lambda i, dest: (0, 0)),
                pl.BlockSpec(memory_space=pl.ANY),
            ],
            out_specs=[
                pl.BlockSpec((tm, d), lambda i, dest: (jnp.minimum(i, na - 1), 0)),
                pl.BlockSpec((tm, d), lambda i, dest: (jnp.maximum(i - na, 0), 0)),
            ],
            scratch_shapes=[pltpu.VMEM((2, TOP_K, tm, d // 2), jnp.uint32), pltpu.SemaphoreType.DMA((2,))],
        ),
        out_shape=[jax.ShapeDtypeStruct((n_first, d), f32), jax.ShapeDtypeStruct((n - n_first, d), f32)],
        compiler_params=_params("arbitrary"),
        name="combine",
    )(dest.reshape(-1), x1, gates, norm_final.reshape(1, d), yb)


def moe_and_final_norm(x1, norm_ffn, w_router, b_router, w1, b1, w2, b2, norm_final, n_first, tb, tf=512, td=512,
                       tm_router=256, tm_combine=128):
    h2, idx, gates = router(x1, norm_ffn, w_router, b_router, tm=tm_router)
    dest, tok, block_e, rows_valid, next_e = _route(idx[:, :TOP_K], w_router.shape[1], tb)
    xg = gather_rows(h2, tok, rows_valid, rb=tb)
    hid = expert_up(xg, w1, b1, block_e, rows_valid, next_e, tb, tf=tf)
    yb = expert_down(hid, w2, b2, block_e, rows_valid, next_e, tb, td=td)
    return combine(x1, gates, norm_final, yb, dest, n_first, td, tm=tm_combine)


def _pick_tile(m, cap=1024):
    units = m // LANES
    best = max(u for u in range(1, cap // LANES + 1) if units % u == 0)
    return best * LANES


def _pow2_chunk(t, cap):
    c = 1
    while c * 2 <= cap and t % (c * 2) == 0:
        c *= 2
    return c


def _state_to_pairs(s):
    B, H = s.shape[:2]
    t = jnp.swapaxes(s, 2, 3).reshape(B, H // 2, 2, HEAD_A, HEAD_A)
    return jnp.swapaxes(t, 2, 3).reshape(B, H // 2, HEAD_A, 2 * HEAD_A)


def _pairs_to_state(hc):
    B, NP = hc.shape[:2]
    t = jnp.swapaxes(hc.reshape(B, NP, HEAD_A, 2, HEAD_A), 2, 3)
    return jnp.swapaxes(t.reshape(B, 2 * NP, HEAD_A, HEAD_A), 2, 3)


def _pad_last(x, width):
    return jnp.pad(x, [(0, 0)] * (x.ndim - 1) + [(0, width - x.shape[-1])])


def kernel(x_prompt, x_sample, state_shift, state_rwkv, state_mlstm_c, state_mlstm_n, state_mlstm_m, norm_mix, w_in, mu_shift, w0, w_up, a0, a_up, g_up, k_k, k_a, r_k, lnx_w, lnx_b, b_igate, b_fgate, mh_norm, b_gate, p_a, p_b, w_out, norm_ffn, w_router, b_router, w_mlp1, b_mlp1, w_mlp2, b_mlp2, norm_final):
    assert norm_mix.shape[0] == 1, "single trunk layer"
    Bp, Tp, D = x_prompt.shape
    Bs, Ts, _ = x_sample.shape
    Np, Ns = Bp * Tp, Bs * Ts
    DA, LW, LA, LG = w0.shape[-1], w_up.shape[1], a_up.shape[1], g_up.shape[1]
    HA = r_k.shape[1]
    assert r_k.shape[2] == HEAD_A and HA * HEAD_A == DA and LG % LANES == 0
    _, _, HB, DK, DV = state_mlstm_c.shape
    DQK, DB = HB * DK, HB * DV
    n_shift = 3 * DA + LW + LA + LG
    n_ml = 2 * DQK + 2 * DB + 2 * HB
    WP, AP = _round_up(LW, LANES), _round_up(LA, LANES)

    def pad_shift_cols(t):
        o = 3 * DA
        return jnp.concatenate(
            [t[..., :o], _pad_last(t[..., o:o + LW], WP), _pad_last(t[..., o + LW:o + LW + LA], AP),
             t[..., o + LW + LA:]], axis=-1)

    def unpad_shift_cols(t):
        o = 3 * DA
        return jnp.concatenate([t[..., :o], t[..., o:o + LW], t[..., o + WP:o + WP + LA], t[..., o + WP + AP:]], axis=-1)

    w = w_in[0]
    wa = pad_shift_cols(w[:, :n_shift]).astype(bf16)
    ob = n_shift + 2 * DQK + 2 * DB
    wb = jnp.concatenate([w[:, n_shift:ob], _pad_last(w[:, ob:n_shift + n_ml], LANES)], axis=-1).astype(bf16)
    wg = w[:, n_shift + n_ml:].astype(bf16)

    xp, xs = x_prompt.reshape(Np, D), x_sample.reshape(Ns, D)
    tm = _pow2_chunk(math.gcd(Np, Ns), 512)
    h = rmsnorm(xp, xs, norm_mix[0], bf16, tm=tm)
    za = matmul(h, wa, _pick_tile(wa.shape[1], 1792), tm=tm)
    zb = matmul(h, wb, _pick_tile(wb.shape[1]), tm=tm)
    zg = matmul(h, wg, _pick_tile(wg.shape[1]), tm=tm)

    row = lambda t: t.reshape(1, -1)
    rwkv_params = (row(pad_shift_cols(mu_shift[0])), row(w0[0]), row(a0[0]), row(k_k[0]), row(k_a[0]), row(r_k[0]),
                   row(lnx_w[0]), row(lnx_b[0]), jnp.pad(w_up[0], ((0, WP - LW), (0, 0))),
                   jnp.pad(a_up[0], ((0, AP - LA), (0, 0))), g_up[0])
    NA = wa.shape[1]

    def rwkv_group(row0, B, T, shift_prev, s0):
        L = _pow2_chunk(T, RWKV_CHUNK)
        S, U = (1, math.gcd(8, DA // LANES)) if T > L else (HEAD_A // L, 4)
        return rwkv(za, row0, B, T, L, S, shift_prev, s0, *rwkv_params, U=U)

    ya_p, s_p, sh_p = rwkv_group(0, Bp, Tp, jnp.zeros((Bp, 1, NA), f32), jnp.zeros((Bp, HA // 2, HEAD_A, LANES), f32))
    ya_s, s_s, sh_s = rwkv_group(Np, Bs, Ts, pad_shift_cols(state_shift[0])[:, None, :],
                                 _state_to_pairs(state_rwkv[0]))
    s_p, s_s = _pairs_to_state(s_p), _pairs_to_state(s_s)

    bias_i = jnp.pad(b_igate[0], (0, LANES - HB)).reshape(1, LANES)
    bias_f = jnp.pad(b_fgate[0], (HB, LANES - 2 * HB)).reshape(1, LANES)
    mh_w = row(mh_norm[0])
    yb_p, c_p, n_p, m_p = mlstm(zb, 0, Bp, Tp, math.gcd(Tp, MLSTM_CHUNK), bias_i, bias_f, mh_w,
                                jnp.zeros((Bp, HB, DK, DV), f32), jnp.zeros((Bp, HB, DK), f32), jnp.zeros((Bp, HB), f32))
    yb_s, c_s, n_s, m_s = mlstm(zb, Np, Bs, Ts, math.gcd(Ts, MLSTM_CHUNK), bias_i, bias_f, mh_w,
                                state_mlstm_c[0], state_mlstm_n[0], state_mlstm_m[0])

    tn = _pick_tile(D, 512)
    u = merge((ya_p, ya_s), (yb_p, yb_s), p_a[0].astype(bf16), p_b[0].astype(bf16), zg, row(b_gate[0]), tm=tm, tn=tn)
    x1 = outproj(u, w_out[0].astype(bf16), xp, xs, tm=tm, tn=tn)
    y_p, y_s = moe_and_final_norm(x1, norm_ffn[0], w_router[0], b_router[0], w_mlp1[0], b_mlp1[0], w_mlp2[0],
                                  b_mlp2[0], norm_final, Np, tb=EXPERT_ROWS, tf=_pick_tile(w_mlp2.shape[2], 512),
                                  td=_pick_tile(D, 1024),
                                  tm_router=min(tm, 256), tm_combine=min(tm, 128))

    shift_p = unpad_shift_cols(sh_p[:, 0])
    shift_s = unpad_shift_cols(sh_s[:, 0])
    return (y_p.reshape(Bp, Tp, D), y_s.reshape(Bs, Ts, D),
            shift_p[None], s_p[None], c_p[None], n_p[None], m_p.reshape(1, Bp, HB),
            shift_s[None], s_s[None], c_s[None], n_s[None], m_s.reshape(1, Bs, HB))
```
